```python
import jax
import jax.numpy as jnp
from jax import lax
import numpy as np

D_MODEL = 1024
BATCH = 8
SEQ = 4096
DEPTH = 2

GRID_W = 64
CTX_LEN = 256
D_MIX = D_MODEL
CONV_DIM = D_MIX // 4
CONV_WIDTH = 3
GMLP_DIM = D_MIX // 4
GMLP_HEADS = 4
GMLP_HEAD_DIM = GMLP_DIM // GMLP_HEADS
CHUNK = 128
RET_DIM = D_MIX // 2
RET_HEADS = 8
RET_HEAD_DIM = RET_DIM // RET_HEADS
RET_SCALE = RET_HEAD_DIM ** -0.5
D_FF = 4 * D_MODEL
N_MOD = 6
LN_EPS = 1e-5
DEEPNORM_ALPHA = (2 * DEPTH) ** 0.25
DEEPNORM_BETA = (8 * DEPTH) ** -0.25

A_X_END = CONV_DIM
A_B_END = 2 * CONV_DIM
A_C_END = 3 * CONV_DIM
B_U_END = A_C_END + GMLP_DIM
B_V_END = B_U_END + GMLP_DIM
Q_END = B_V_END + RET_DIM
K_END = Q_END + RET_DIM
V_END = K_END + RET_DIM
D_IN = V_END + RET_DIM
SPLIT_POINTS = (A_X_END, A_B_END, A_C_END, B_U_END, B_V_END, Q_END, K_END, V_END)

kernel_name = 'hybrid_conv_gmlp_retention_dit'


def layer_norm(x, g, b):
    xf = x.astype(jnp.float32)
    mu = jnp.mean(xf, axis=-1, keepdims=True)
    var = jnp.mean(jnp.square(xf - mu), axis=-1, keepdims=True)
    y = (xf - mu) * lax.rsqrt(var + LN_EPS) * g.astype(jnp.float32) + b.astype(jnp.float32)
    return y.astype(x.dtype)


def ada_modulation(cond, w_ada, b_ada):
    m = jax.nn.silu(cond) @ w_ada + b_ada
    return jnp.split(m, N_MOD, axis=-1)


def modulate(x, shift, scale):
    return x * (1 + scale) + shift


def centred_conv3(z, w):
    n = z.shape[-2]
    zp = jnp.pad(z, [(0, 0)] * (z.ndim - 2) + [(1, 1), (0, 0)])
    return w[0] * zp[..., 0:n, :] + w[1] * zp[..., 1:n + 1, :] + w[2] * zp[..., 2:n + 2, :]


def conv_mixer(xin, gate_b, gate_c, w_conv, rows):
    z = gate_c * xin
    if rows is None:
        z = centred_conv3(z, w_conv)
    else:
        bsz, n, ch = z.shape
        z = centred_conv3(z.reshape(bsz, rows, GRID_W, ch), w_conv).reshape(bsz, n, ch)
    return gate_b * z


def chunk_mlp(u, v, ln_g, ln_b, w_s, b_s):
    u = jax.nn.gelu(u)
    v = layer_norm(jax.nn.gelu(v), ln_g, ln_b)
    bsz, n, _ = v.shape
    vc = v.reshape(bsz, n // CHUNK, CHUNK, GMLP_HEADS, GMLP_HEAD_DIM)
    mixed = jnp.einsum('hpq,bcqhd->bcphd', w_s, vc) + b_s.T[None, None, :, :, None]
    return u * mixed.reshape(bsz, n, GMLP_DIM)


def ret_heads(t):
    bsz, n, _ = t.shape
    return t.reshape(bsz, n, RET_HEADS, RET_HEAD_DIM).astype(jnp.float32)


def retention_chunked(q, k, v, log_gamma, s0, include_diag):
    bsz, n, h, d = q.shape
    nc = n // CHUNK
    pos = jnp.arange(CHUNK, dtype=jnp.float32)
    rel = pos[:, None] - pos[None, :]
    mask = (rel >= 0) if include_diag else (rel > 0)
    lg = log_gamma[:, None, None]
    decay_in = jnp.where(mask, jnp.exp(lg * jnp.where(mask, rel, 0.0)), 0.0)
    decay_q = jnp.exp(log_gamma[:, None] * (pos + 1.0))[None, :, :, None]
    decay_k = jnp.exp(log_gamma[:, None] * (CHUNK - 1.0 - pos))[None, :, :, None]
    decay_chunk = jnp.exp(log_gamma * CHUNK)[None, :, None, None]

    def to_chunks(t):
        return t.reshape(bsz, nc, CHUNK, h, d).transpose(1, 0, 3, 2, 4)

    def step(s, qkv):
        qc, kc, vc = qkv
        scores = jnp.einsum('bhid,bhjd->bhij', qc, kc) * decay_in
        o = jnp.einsum('bhij,bhje->bhie', scores, vc) + jnp.einsum('bhid,bhde->bhie', qc * decay_q, s)
        s = s * decay_chunk + jnp.einsum('bhjd,bhje->bhde', kc * decay_k, vc)
        return s, o

    _, o = lax.scan(step, s0, (to_chunks(q), to_chunks(k), to_chunks(v)))
    return o.transpose(1, 0, 3, 2, 4).reshape(bsz, n, h, d)


def context_states(k, v, lg_f, lg_b):
    n = k.shape[1]
    pos = jnp.arange(n, dtype=jnp.float32)
    w_f = jnp.exp(lg_f[None, :] * (n - 1.0 - pos)[:, None])
    w_b = jnp.exp(lg_b[None, :] * pos[:, None])
    s_f = jnp.einsum('blhd,blhe->bhde', k * w_f[None, :, :, None], v)
    s_b = jnp.einsum('blhd,blhe->bhde', k * w_b[None, :, :, None], v)
    return s_f, s_b


def retention_mixer(q, k, v, g, lg_f, lg_b, s_f, s_b):
    bsz, n, _ = q.shape
    q, k, v = ret_heads(q) * RET_SCALE, ret_heads(k), ret_heads(v)
    o_f = retention_chunked(q, k, v, lg_f, s_f, True)
    o_b = jnp.flip(retention_chunked(jnp.flip(q, 1), jnp.flip(k, 1), jnp.flip(v, 1), lg_b, s_b, False), 1)
    o = o_f + o_b
    mu = jnp.mean(o, axis=-1, keepdims=True)
    var = jnp.mean(jnp.square(o - mu), axis=-1, keepdims=True)
    o = (o - mu) * lax.rsqrt(var + LN_EPS)
    return jax.nn.silu(g) * o.reshape(bsz, n, RET_DIM).astype(g.dtype)


def token_mixers(proj, w_conv, gm_ln_g, gm_ln_b, gm_ws, gm_bs, lg_f, lg_b, s_f, s_b, rows):
    xin, gate_b, gate_c, u, v, q, k, vr, g = jnp.split(proj, SPLIT_POINTS, axis=-1)
    y_conv = conv_mixer(xin, gate_b, gate_c, w_conv, rows)
    y_gmlp = chunk_mlp(u, v, gm_ln_g, gm_ln_b, gm_ws, gm_bs)
    y_ret = retention_mixer(q, k, vr, g, lg_f, lg_b, s_f, s_b)
    return jnp.concatenate([y_conv, y_gmlp, y_ret], axis=-1)


def sq_relu_mlp(h, w1, w2):
    return jnp.square(jax.nn.relu(h @ w1)) @ w2


def setup_inputs(seed: int = 0) -> dict:
    key = jax.random.key(seed)
    ks = jax.random.split(key, 19)

    def nrm(k, shape, s):
        return jax.random.normal(k, shape, jnp.float32) * s

    base_decay = jnp.log(-jnp.log1p(-(2.0 ** (-5.0 - jnp.arange(RET_HEADS, dtype=jnp.float32)))))
    return {
        'x': nrm(ks[0], (BATCH, SEQ, D_MODEL), 1.0),
        'c': nrm(ks[1], (BATCH, D_MODEL), 1.0),
        'ctx': nrm(ks[2], (BATCH, CTX_LEN, D_MODEL), 1.0),
        'c_ctx': nrm(ks[3], (D_MODEL,), 1.0),
        'w_ada': nrm(ks[4], (DEPTH, D_MODEL, N_MOD * D_MODEL), 0.5 * D_MODEL ** -0.5),
        'b_ada': nrm(ks[5], (DEPTH, N_MOD * D_MODEL), 0.02),
        'w_in': nrm(ks[6], (DEPTH, D_MODEL, D_IN), D_MODEL ** -0.5),
        'conv_w': nrm(ks[7], (DEPTH, CONV_WIDTH, CONV_DIM), CONV_WIDTH ** -0.5),
        'gmlp_ln_g': 1.0 + nrm(ks[8], (DEPTH, GMLP_DIM), 0.02),
        'gmlp_ln_b': nrm(ks[9], (DEPTH, GMLP_DIM), 0.02),
        'gmlp_ws': nrm(ks[10], (DEPTH, GMLP_HEADS, CHUNK, CHUNK), CHUNK ** -0.5),
        'gmlp_bs': 1.0 + nrm(ks[11], (DEPTH, GMLP_HEADS, CHUNK), 0.02),
        'ret_decay_fwd': base_decay + nrm(ks[12], (DEPTH, RET_HEADS), 0.01),
        'ret_decay_bwd': base_decay + nrm(ks[13], (DEPTH, RET_HEADS), 0.01),
        'w_out': nrm(ks[14], (DEPTH, D_MIX, D_MODEL), DEEPNORM_BETA * D_MIX ** -0.5),
        'w_ff1': nrm(ks[15], (DEPTH, D_MODEL, D_FF), D_MODEL ** -0.5),
        'w_ff2': nrm(ks[16], (DEPTH, D_FF, D_MODEL), DEEPNORM_BETA * D_FF ** -0.5),
        'ln_g': 1.0 + nrm(ks[17], (DEPTH, 2, D_MODEL), 0.02),
        'ln_b': nrm(ks[18], (DEPTH, 2, D_MODEL), 0.02),
    }


def reference(x, c, ctx, c_ctx, w_ada, b_ada, w_in, conv_w, gmlp_ln_g, gmlp_ln_b, gmlp_ws, gmlp_bs,
              ret_decay_fwd, ret_decay_bwd, w_out, w_ff1, w_ff2, ln_g, ln_b):
    bsz, n_lat, _ = x.shape
    rows = n_lat // GRID_W
    xc = ctx
    zero_state = jnp.zeros((bsz, RET_HEADS, RET_HEAD_DIM, RET_HEAD_DIM), jnp.float32)
    for l in range(DEPTH):
        last = l == DEPTH - 1
        sh_m, sc_m, gt_m, sh_f, sc_f, gt_f = [m[:, None, :] for m in ada_modulation(c, w_ada[l], b_ada[l])]
        csh_m, csc_m, cgt_m, csh_f, csc_f, cgt_f = ada_modulation(c_ctx, w_ada[l], b_ada[l])
        lg_f = -jnp.exp(ret_decay_fwd[l].astype(jnp.float32))
        lg_b = -jnp.exp(ret_decay_bwd[l].astype(jnp.float32))

        hc = modulate(xc, csh_m, csc_m)
        if last:
            kv_c = hc @ w_in[l][:, Q_END:V_END]
        else:
            proj_c = hc @ w_in[l]
            kv_c = proj_c[..., Q_END:V_END]
        k_c, v_c = jnp.split(kv_c, 2, axis=-1)
        s_f, s_b = context_states(ret_heads(k_c), ret_heads(v_c), lg_f, lg_b)

        h = modulate(x, sh_m, sc_m)
        y = token_mixers(h @ w_in[l], conv_w[l], gmlp_ln_g[l], gmlp_ln_b[l], gmlp_ws[l], gmlp_bs[l],
                         lg_f, lg_b, s_f, s_b, rows)
        x = layer_norm(DEEPNORM_ALPHA * x + gt_m * (y @ w_out[l]), ln_g[l, 0], ln_b[l, 0])
        y = sq_relu_mlp(modulate(x, sh_f, sc_f), w_ff1[l], w_ff2[l])
        x = layer_norm(DEEPNORM_ALPHA * x + gt_f * y, ln_g[l, 1], ln_b[l, 1])

        if not last:
            yc = token_mixers(proj_c, conv_w[l], gmlp_ln_g[l], gmlp_ln_b[l], gmlp_ws[l], gmlp_bs[l],
                              lg_f, lg_b, zero_state, zero_state, None)
            xc = layer_norm(DEEPNORM_ALPHA * xc + cgt_m * (yc @ w_out[l]), ln_g[l, 0], ln_b[l, 0])
            yc = sq_relu_mlp(modulate(xc, csh_f, csc_f), w_ff1[l], w_ff2[l])
            xc = layer_norm(DEEPNORM_ALPHA * xc + cgt_f * yc, ln_g[l, 1], ln_b[l, 1])
    return x
```

```python
import functools

import jax
import jax.numpy as jnp
from jax import lax
from jax.experimental import pallas as pl
from jax.experimental.pallas import tpu as pltpu

D_MODEL = 1024
DEPTH = 2
GRID_W = 64
CONV_DIM = 256
GMLP_DIM = 256
GMLP_HEADS = 4
CHUNK = 128
RET_DIM = 512
RET_HEADS = 8
RET_HEAD_DIM = 64
RET_SCALE = RET_HEAD_DIM ** -0.5
D_FF = 4 * D_MODEL
N_MOD = 6
LN_EPS = 1e-5
ALPHA = (2 * DEPTH) ** 0.25
D_IN = 3 * CONV_DIM + 2 * GMLP_DIM + 4 * RET_DIM

C_CONV = 0
C_GMLP = 3 * CONV_DIM
C_Q = C_GMLP + 2 * GMLP_DIM
C_K = C_Q + RET_DIM
C_V = C_K + RET_DIM
C_G = C_V + RET_DIM

N_PAIR = RET_HEADS // 2
PAIR_W = 2 * RET_HEAD_DIM
MOD_ROWS = 16
CTX_MOD_ROW = 8
FF_BLOCK = 1024

VMEM_LIMIT_BYTES = 56 * 1024 * 1024

F32 = jnp.float32
BF16 = jnp.bfloat16


def _layer_norm(v, g, b):
    mu = jnp.mean(v, axis=-1, keepdims=True)
    d = v - mu
    var = jnp.mean(d * d, axis=-1, keepdims=True)
    return d * lax.rsqrt(var + LN_EPS) * g + b


def _mm(a, b):
    return jnp.dot(a, b, preferred_element_type=F32)


def _mm_nt(a, b):
    return lax.dot_general(a, b, (((1,), (1,)), ((), ())), preferred_element_type=F32)


def _mm_tn(a, b):
    return lax.dot_general(a, b, (((0,), (0,)), ((), ())), preferred_element_type=F32)


def _pair_masks():
    lane = lax.broadcasted_iota(jnp.int32, (1, PAIR_W), 1)
    row = lax.broadcasted_iota(jnp.int32, (PAIR_W, 1), 0)
    lo_lane = lane < RET_HEAD_DIM
    block_diag = (row < RET_HEAD_DIM) == lo_lane
    return lo_lane, block_diag


def _stack_heads(a, lo_lane):
    zero = jnp.zeros_like(a)
    return jnp.concatenate([jnp.where(lo_lane, a, zero), jnp.where(lo_lane, zero, a)], axis=0)


def _mod_kernel(cc_ref, w_ref, b_ref, o_ref):
    s = jax.nn.silu(cc_ref[...]).astype(BF16)
    o_ref[...] = _mm(s, w_ref[...].astype(BF16)) + b_ref[...]


def _run_mod(cc, w_ada, b_ada):
    nblk = (N_MOD * D_MODEL) // D_MODEL
    return pl.pallas_call(
        _mod_kernel,
        grid=(DEPTH, nblk),
        in_specs=[
            pl.BlockSpec((MOD_ROWS, D_MODEL), lambda l, j: (0, 0)),
            pl.BlockSpec((None, D_MODEL, D_MODEL), lambda l, j: (l, 0, j)),
            pl.BlockSpec((None, 1, D_MODEL), lambda l, j: (l, 0, j)),
        ],
        out_specs=pl.BlockSpec((None, MOD_ROWS, D_MODEL), lambda l, j: (l, 0, j)),
        out_shape=jax.ShapeDtypeStruct((DEPTH, MOD_ROWS, N_MOD * D_MODEL), F32),
        compiler_params=pltpu.CompilerParams(dimension_semantics=("arbitrary", "arbitrary")),
        name="ada_modulation",
    )(cc, w_ada, b_ada)


def _mix_in_kernel(x_ref, mod_ref, win_ref, convw_ref, glng_ref, glnb_ref, ws01_ref, ws23_ref, bs_ref,
                   decf_ref, decb_ref, sb0_ref,
                   ycg_ref, qkv_ref, g_ref, sb_ref, *rest, tile, conv_row, emit_states):
    if emit_states:
        sf_out_ref, sb_out_ref, sb_scr = rest
    else:
        (sb_scr,) = rest
    n_chunk = tile // CHUNK
    t = pl.program_id(1)

    @pl.when(t == 0)
    def _():
        sb_scr[...] = sb0_ref[...]

    sh_m = mod_ref[:, 0 * D_MODEL:1 * D_MODEL]
    sc_m = mod_ref[:, 1 * D_MODEL:2 * D_MODEL]
    h = (x_ref[...] * (1.0 + sc_m) + sh_m).astype(BF16)

    pa = _mm(h, win_ref[:, C_CONV:C_CONV + 3 * CONV_DIM])
    xin = pa[:, 0:CONV_DIM]
    gate_b = pa[:, CONV_DIM:2 * CONV_DIM]
    gate_c = pa[:, 2 * CONV_DIM:3 * CONV_DIM]
    z = gate_c * xin
    pos_in_row = lax.broadcasted_iota(jnp.int32, (tile, 1), 0) & (conv_row - 1)
    z_prev = jnp.where(pos_in_row == 0, 0.0, pltpu.roll(z, 1, 0))
    z_next = jnp.where(pos_in_row == conv_row - 1, 0.0, pltpu.roll(z, tile - 1, 0))
    y_conv = gate_b * (convw_ref[0:1, :] * z_prev + convw_ref[1:2, :] * z + convw_ref[2:3, :] * z_next)
    ycg_ref[:, 0:CONV_DIM] = y_conv.astype(BF16)

    pb = _mm(h, win_ref[:, C_GMLP:C_GMLP + 2 * GMLP_DIM])
    u_act = jax.nn.gelu(pb[:, 0:GMLP_DIM])
    v_ln = _layer_norm(jax.nn.gelu(pb[:, GMLP_DIM:2 * GMLP_DIM]), glng_ref[...], glnb_ref[...]).astype(BF16)
    lane = lax.broadcasted_iota(jnp.int32, (1, GMLP_DIM), 1)
    head_w = GMLP_DIM // GMLP_HEADS
    in_head = [(lane >= hh * head_w) & (lane < (hh + 1) * head_w) for hh in range(GMLP_HEADS)]
    zero_bf = jnp.zeros((CHUNK, GMLP_DIM), BF16)
    for c in range(n_chunk):
        rows = slice(c * CHUNK, (c + 1) * CHUNK)
        vc = v_ln[rows, :]
        by_head = [jnp.where(in_head[hh], vc, zero_bf) for hh in range(GMLP_HEADS)]
        mixed = (_mm(ws01_ref[...], jnp.concatenate(by_head[0:2], axis=0))
                 + _mm(ws23_ref[...], jnp.concatenate(by_head[2:4], axis=0)) + bs_ref[...])
        ycg_ref[rows, CONV_DIM:CONV_DIM + GMLP_DIM] = (u_act[rows, :] * mixed).astype(BF16)

    q = _mm(h, win_ref[:, C_Q:C_Q + RET_DIM]) * RET_SCALE
    qkv_ref[:, 0:RET_DIM] = q.astype(BF16)
    k = _mm(h, win_ref[:, C_K:C_K + RET_DIM])
    qkv_ref[:, RET_DIM:2 * RET_DIM] = k.astype(BF16)
    v = _mm(h, win_ref[:, C_V:C_V + RET_DIM])
    v_bf = v.astype(BF16)
    qkv_ref[:, 2 * RET_DIM:3 * RET_DIM] = v_bf
    g_ref[...] = _mm(h, win_ref[:, C_G:C_G + RET_DIM])

    _, block_diag = _pair_masks()
    pos = lax.broadcasted_iota(jnp.int32, (CHUNK, RET_DIM), 0).astype(F32)
    lg_b = -jnp.exp(decb_ref[...])
    w_b = jnp.exp(lg_b * pos)
    decay_chunk_b = jnp.exp(lg_b * float(CHUNK))
    if emit_states:
        lg_f = -jnp.exp(decf_ref[...])
        w_f = jnp.exp(lg_f * (float(CHUNK - 1) - pos))
        decay_chunk_f = jnp.exp(lg_f * float(CHUNK))
    for c in reversed(range(n_chunk)):
        rows = slice(c * CHUNK, (c + 1) * CHUNK)
        for p in range(N_PAIR):
            cols = slice(p * PAIR_W, (p + 1) * PAIR_W)
            prow = slice(p * PAIR_W, (p + 1) * PAIR_W)
            s_old = sb_scr[prow, :]
            sb_ref[c, prow, :] = s_old.astype(BF16)
            kw = (k[rows, cols] * w_b[:, cols]).astype(BF16)
            upd = _mm_tn(kw, v_bf[rows, cols])
            sb_scr[prow, :] = s_old * decay_chunk_b[:, cols] + jnp.where(block_diag, upd, 0.0)

    if emit_states:
        sb_out_ref[...] = sb_scr[...]
        for p in range(N_PAIR):
            cols = slice(p * PAIR_W, (p + 1) * PAIR_W)
            s_f = jnp.zeros((PAIR_W, PAIR_W), F32)
            for c in range(n_chunk):
                rows = slice(c * CHUNK, (c + 1) * CHUNK)
                kw = (k[rows, cols] * w_f[:, cols]).astype(BF16)
                upd = _mm_tn(kw, v_bf[rows, cols])
                s_f = s_f * decay_chunk_f[:, cols] + jnp.where(block_diag, upd, 0.0)
            sf_out_ref[cols, :] = s_f


def _run_mix_in(x, mod_l, win, convw, glng, glnb, ws01, ws23, bs_tile, decf, decb, sb0, *,
                tile, conv_row, mod_row, emit_states):
    bsz, n, _ = x.shape
    nt = n // tile
    n_chunk = tile // CHUNK
    if emit_states:
        assert nt == 1
    rev = lambda b, t: (b, nt - 1 - t, 0)
    const2 = lambda b, t: (0, 0)
    mod_map = (lambda b, t: (b, 0, 0)) if mod_row is None else (lambda b, t: (mod_row, 0, 0))
    state_map = lambda b, t: (b, 0, 0)
    out_shape = [
        jax.ShapeDtypeStruct((bsz, n, CONV_DIM + GMLP_DIM), BF16),
        jax.ShapeDtypeStruct((bsz, n, 3 * RET_DIM), BF16),
        jax.ShapeDtypeStruct((bsz, n, RET_DIM), F32),
        jax.ShapeDtypeStruct((bsz, n // CHUNK, RET_DIM, PAIR_W), BF16),
    ]
    out_specs = [
        pl.BlockSpec((None, tile, CONV_DIM + GMLP_DIM), rev),
        pl.BlockSpec((None, tile, 3 * RET_DIM), rev),
        pl.BlockSpec((None, tile, RET_DIM), rev),
        pl.BlockSpec((None, n_chunk, RET_DIM, PAIR_W), lambda b, t: (b, nt - 1 - t, 0, 0)),
    ]
    if emit_states:
        out_shape += [jax.ShapeDtypeStruct((bsz, RET_DIM, PAIR_W), F32)] * 2
        out_specs += [pl.BlockSpec((None, RET_DIM, PAIR_W), state_map)] * 2
    return pl.pallas_call(
        functools.partial(_mix_in_kernel, tile=tile, conv_row=conv_row, emit_states=emit_states),
        grid=(bsz, nt),
        in_specs=[
            pl.BlockSpec((None, tile, D_MODEL), rev),
            pl.BlockSpec((None, 1, N_MOD * D_MODEL), mod_map),
            pl.BlockSpec((D_MODEL, D_IN), const2),
            pl.BlockSpec((3, CONV_DIM), const2),
            pl.BlockSpec((1, GMLP_DIM), const2),
            pl.BlockSpec((1, GMLP_DIM), const2),
            pl.BlockSpec((CHUNK, 2 * CHUNK), const2),
            pl.BlockSpec((CHUNK, 2 * CHUNK), const2),
            pl.BlockSpec((CHUNK, GMLP_DIM), const2),
            pl.BlockSpec((1, RET_DIM), const2),
            pl.BlockSpec((1, RET_DIM), const2),
            pl.BlockSpec((None, RET_DIM, PAIR_W), state_map),
        ],
        out_specs=out_specs,
        out_shape=out_shape,
        scratch_shapes=[pltpu.VMEM((RET_DIM, PAIR_W), F32)],
        compiler_params=pltpu.CompilerParams(
            dimension_semantics=("arbitrary", "arbitrary"), vmem_limit_bytes=VMEM_LIMIT_BYTES),
        name="mix_in_ctx" if emit_states else "mix_in",
    )(x, mod_l, win, convw, glng, glnb, ws01, ws23, bs_tile, decf, decb, sb0)


def _mix_out_kernel(x_ref, ycg_ref, qkv_ref, g_ref, sb_ref, sf0_ref, mod_ref, lng_ref, lnb_ref,
                    wout_ref, wff1_ref, wff2_ref, decf_ref, decb_ref, decf2_ref, decb2_ref,
                    o_ref, sf_scr, d2_scr, wq_scr, y_scr, hid_scr, *, tile):
    n_chunk = tile // CHUNK
    b = pl.program_id(0)
    t = pl.program_id(1)

    lg_f = -jnp.exp(decf_ref[...])
    lg_b = -jnp.exp(decb_ref[...])

    @pl.when((b == 0) & (t == 0))
    def _():
        pos = lax.broadcasted_iota(jnp.int32, (CHUNK, RET_DIM), 0).astype(F32)
        wq_scr[0] = jnp.exp(lg_f * (pos + 1.0))
        wq_scr[1] = jnp.exp(lg_b * (float(CHUNK) - pos))
        wq_scr[2] = jnp.exp(lg_f * (float(CHUNK - 1) - pos))
        i = lax.broadcasted_iota(jnp.int32, (CHUNK, 2 * CHUNK), 0)
        j = lax.broadcasted_iota(jnp.int32, (CHUNK, 2 * CHUNK), 1) & (CHUNK - 1)
        rel = i - j
        causal = rel >= 0
        relf = rel.astype(F32)
        for p in range(N_PAIR):
            lf = -jnp.exp(decf2_ref[p])
            lb = -jnp.exp(decb2_ref[p])
            d2_scr[p] = jnp.where(causal, jnp.exp(lf * jnp.where(causal, relf, 0.0)),
                                  jnp.exp(lb * jnp.where(causal, 0.0, -relf)))

    @pl.when(t == 0)
    def _():
        sf_scr[...] = sf0_ref[...]

    lo_lane, block_diag = _pair_masks()
    decay_chunk_f = jnp.exp(lg_f * float(CHUNK))

    y_scr[:, 0:CONV_DIM + GMLP_DIM] = ycg_ref[...]
    for c in range(n_chunk):
        rows = slice(c * CHUNK, (c + 1) * CHUNK)
        for p in range(N_PAIR):
            cols = slice(p * PAIR_W, (p + 1) * PAIR_W)
            prow = slice(p * PAIR_W, (p + 1) * PAIR_W)
            qp = qkv_ref[rows, p * PAIR_W:(p + 1) * PAIR_W]
            kp = qkv_ref[rows, RET_DIM + p * PAIR_W:RET_DIM + (p + 1) * PAIR_W]
            vp = qkv_ref[rows, 2 * RET_DIM + p * PAIR_W:2 * RET_DIM + (p + 1) * PAIR_W]
            scores = _mm_nt(qp, _stack_heads(kp, lo_lane))
            probs = (scores * d2_scr[p]).astype(BF16)
            o = _mm(probs, _stack_heads(vp, lo_lane))
            s_f = sf_scr[prow, :]
            o = o + wq_scr[0, :, cols] * _mm(qp, s_f.astype(BF16))
            o = o + wq_scr[1, :, cols] * _mm(qp, sb_ref[c, prow, :])
            kw = (kp.astype(F32) * wq_scr[2, :, cols]).astype(BF16)
            sf_scr[prow, :] = s_f * decay_chunk_f[:, cols] + jnp.where(block_diag, _mm_tn(kw, vp), 0.0)
            inv_d = 1.0 / RET_HEAD_DIM
            mu = jnp.where(lo_lane,
                           jnp.sum(jnp.where(lo_lane, o, 0.0), axis=-1, keepdims=True),
                           jnp.sum(jnp.where(lo_lane, 0.0, o), axis=-1, keepdims=True)) * inv_d
            d = o - mu
            dd = d * d
            var = jnp.where(lo_lane,
                            jnp.sum(jnp.where(lo_lane, dd, 0.0), axis=-1, keepdims=True),
                            jnp.sum(jnp.where(lo_lane, 0.0, dd), axis=-1, keepdims=True)) * inv_d
            o_n = d * lax.rsqrt(var + LN_EPS)
            y_scr[rows, RET_DIM + p * PAIR_W:RET_DIM + (p + 1) * PAIR_W] = (
                jax.nn.silu(g_ref[rows, cols]) * o_n).astype(BF16)

    gt_m = mod_ref[:, 2 * D_MODEL:3 * D_MODEL]
    sh_f = mod_ref[:, 3 * D_MODEL:4 * D_MODEL]
    sc_f = mod_ref[:, 4 * D_MODEL:5 * D_MODEL]
    gt_f = mod_ref[:, 5 * D_MODEL:6 * D_MODEL]
    x = x_ref[...]
    x1 = _layer_norm(ALPHA * x + gt_m * _mm(y_scr[...], wout_ref[...]), lng_ref[0:1, :], lnb_ref[0:1, :])
    hf = (x1 * (1.0 + sc_f) + sh_f).astype(BF16)
    for nb in range(D_FF // FF_BLOCK):
        cols = slice(nb * FF_BLOCK, (nb + 1) * FF_BLOCK)
        a = jnp.maximum(_mm(hf, wff1_ref[:, cols]), 0.0)
        hid_scr[:, cols] = (a * a).astype(BF16)
    y_ff = _mm(hid_scr[...], wff2_ref[...])
    o_ref[...] = _layer_norm(ALPHA * x1 + gt_f * y_ff, lng_ref[1:2, :], lnb_ref[1:2, :])


def _run_mix_out(x, ycg, qkv, g, sb, sf0, mod_l, lng, lnb, wout, wff1, wff2, decf, decb, decf2, decb2, *,
                 tile, mod_row):
    bsz, n, _ = x.shape
    nt = n // tile
    n_chunk = tile // CHUNK
    fwd = lambda b, t: (b, t, 0)
    const2 = lambda b, t: (0, 0)
    const3 = lambda b, t: (0, 0, 0)
    mod_map = (lambda b, t: (b, 0, 0)) if mod_row is None else (lambda b, t: (mod_row, 0, 0))
    resident = dict(pipeline_mode=pl.Buffered(1))
    return pl.pallas_call(
        functools.partial(_mix_out_kernel, tile=tile),
        grid=(bsz, nt),
        in_specs=[
            pl.BlockSpec((None, tile, D_MODEL), fwd),
            pl.BlockSpec((None, tile, CONV_DIM + GMLP_DIM), fwd),
            pl.BlockSpec((None, tile, 3 * RET_DIM), fwd),
            pl.BlockSpec((None, tile, RET_DIM), fwd),
            pl.BlockSpec((None, n_chunk, RET_DIM, PAIR_W), lambda b, t: (b, t, 0, 0)),
            pl.BlockSpec((None, RET_DIM, PAIR_W), lambda b, t: (b, 0, 0)),
            pl.BlockSpec((None, 1, N_MOD * D_MODEL), mod_map),
            pl.BlockSpec((2, D_MODEL), const2),
            pl.BlockSpec((2, D_MODEL), const2),
            pl.BlockSpec((D_MODEL, D_MODEL), const2, **resident),
            pl.BlockSpec((D_MODEL, D_FF), const2, **resident),
            pl.BlockSpec((D_FF, D_MODEL), const2, **resident),
            pl.BlockSpec((1, RET_DIM), const2),
            pl.BlockSpec((1, RET_DIM), const2),
            pl.BlockSpec((N_PAIR, 1, 2 * CHUNK), const3),
            pl.BlockSpec((N_PAIR, 1, 2 * CHUNK), const3),
        ],
        out_specs=pl.BlockSpec((None, tile, D_MODEL), fwd),
        out_shape=jax.ShapeDtypeStruct((bsz, n, D_MODEL), F32),
        scratch_shapes=[
            pltpu.VMEM((RET_DIM, PAIR_W), F32),
            pltpu.VMEM((N_PAIR, CHUNK, 2 * CHUNK), F32),
            pltpu.VMEM((3, CHUNK, RET_DIM), F32),
            pltpu.VMEM((tile, D_MODEL), BF16),
            pltpu.VMEM((tile, D_FF), BF16),
        ],
        compiler_params=pltpu.CompilerParams(
            dimension_semantics=("arbitrary", "arbitrary"), vmem_limit_bytes=VMEM_LIMIT_BYTES),
        name="mix_out",
    )(x, ycg, qkv, g, sb, sf0, mod_l, lng, lnb, wout, wff1, wff2, decf, decb, decf2, decb2)


def kernel(x, c, ctx, c_ctx, w_ada, b_ada, w_in, conv_w, gmlp_ln_g, gmlp_ln_b, gmlp_ws, gmlp_bs,
           ret_decay_fwd, ret_decay_bwd, w_out, w_ff1, w_ff2, ln_g, ln_b):
    bsz, n_lat, _ = x.shape
    ctx_len = ctx.shape[1]
    lat_tile = 256
    assert n_lat % lat_tile == 0 and lat_tile % CHUNK == 0 and ctx_len % CHUNK == 0
    assert ctx_len & (ctx_len - 1) == 0 and GRID_W & (GRID_W - 1) == 0

    win_b = w_in.astype(BF16)
    wout_b = w_out.astype(BF16)
    wff1_b = w_ff1.astype(BF16)
    wff2_b = w_ff2.astype(BF16)

    cc = jnp.concatenate([c, c_ctx[None, :], jnp.zeros((MOD_ROWS - bsz - 1, D_MODEL), F32)], axis=0)
    mod = _run_mod(cc, w_ada, b_ada.reshape(DEPTH, 1, N_MOD * D_MODEL))
    mod = mod.reshape(DEPTH, MOD_ROWS, 1, N_MOD * D_MODEL)

    zero_state = jnp.zeros((bsz, RET_DIM, PAIR_W), F32)
    xc = ctx
    for l in range(DEPTH):
        last = l == DEPTH - 1
        ws = gmlp_ws[l].astype(BF16)
        ws01 = jnp.concatenate([ws[0], ws[1]], axis=1)
        ws23 = jnp.concatenate([ws[2], ws[3]], axis=1)
        bs_tile = jnp.repeat(gmlp_bs[l].T, GMLP_DIM // GMLP_HEADS, axis=1)
        decf = jnp.repeat(ret_decay_fwd[l].astype(F32), RET_HEAD_DIM)[None, :]
        decb = jnp.repeat(ret_decay_bwd[l].astype(F32), RET_HEAD_DIM)[None, :]
        decf2 = jnp.repeat(ret_decay_fwd[l].astype(F32), CHUNK).reshape(N_PAIR, 1, 2 * CHUNK)
        decb2 = jnp.repeat(ret_decay_bwd[l].astype(F32), CHUNK).reshape(N_PAIR, 1, 2 * CHUNK)
        mix_consts = (mod[l], win_b[l], conv_w[l], gmlp_ln_g[l][None, :], gmlp_ln_b[l][None, :],
                      ws01, ws23, bs_tile, decf, decb)
        out_consts = (mod[l], ln_g[l], ln_b[l], wout_b[l], wff1_b[l], wff2_b[l], decf, decb, decf2, decb2)

        ycg_c, qkv_c, g_c, sb_c, sf_ctx, sb_ctx = _run_mix_in(
            xc, *mix_consts, zero_state, tile=ctx_len, conv_row=ctx_len, mod_row=CTX_MOD_ROW, emit_states=True)
        ycg, qkv, g, sb = _run_mix_in(
            x, *mix_consts, sb_ctx, tile=lat_tile, conv_row=GRID_W, mod_row=None, emit_states=False)
        x = _run_mix_out(x, ycg, qkv, g, sb, sf_ctx, *out_consts, tile=lat_tile, mod_row=None)
        if not last:
            xc = _run_mix_out(xc, ycg_c, qkv_c, g_c, sb_c, zero_state, *out_consts,
                              tile=ctx_len, mod_row=CTX_MOD_ROW)
    return x
```

```python
import functools

import jax
import jax.numpy as jnp
from jax import lax
from jax.experimental import pallas as pl
from jax.experimental.pallas import tpu as pltpu

D_MODEL = 1024
DEPTH = 2
GRID_W = 64
CONV_DIM = 256
GMLP_DIM = 256
GMLP_HEADS = 4
CHUNK = 128
RET_DIM = 512
RET_HEADS = 8
RET_HEAD_DIM = 64
RET_SCALE = RET_HEAD_DIM ** -0.5
D_FF = 4 * D_MODEL
N_MOD = 6
LN_EPS = 1e-5
ALPHA = (2 * DEPTH) ** 0.25
D_IN = 3 * CONV_DIM + 2 * GMLP_DIM + 4 * RET_DIM

C_CONV = 0
C_GMLP = 3 * CONV_DIM
C_Q = C_GMLP + 2 * GMLP_DIM
C_K = C_Q + RET_DIM
C_V = C_K + RET_DIM
C_G = C_V + RET_DIM

N_PAIR = RET_HEADS // 2
PAIR_W = 2 * RET_HEAD_DIM
MOD_ROWS = 16
CTX_MOD_ROW = 8
FF_BLOCK = 1024
SUB_ROWS = 256

VMEM_LIMIT_BYTES = 56 * 1024 * 1024

F32 = jnp.float32
BF16 = jnp.bfloat16


def _layer_norm(v, g, b):
    mu = jnp.mean(v, axis=-1, keepdims=True)
    d = v - mu
    var = jnp.mean(d * d, axis=-1, keepdims=True)
    return d * lax.rsqrt(var + LN_EPS) * g + b


def _mm(a, b):
    return jnp.dot(a, b, preferred_element_type=F32)


def _mm_nt(a, b):
    return lax.dot_general(a, b, (((1,), (1,)), ((), ())), preferred_element_type=F32)


def _mm_tn(a, b):
    return lax.dot_general(a, b, (((0,), (0,)), ((), ())), preferred_element_type=F32)


def _pair_masks():
    lane = lax.broadcasted_iota(jnp.int32, (1, PAIR_W), 1)
    row = lax.broadcasted_iota(jnp.int32, (PAIR_W, 1), 0)
    lo_lane = lane < RET_HEAD_DIM
    block_diag = (row < RET_HEAD_DIM) == lo_lane
    return lo_lane, block_diag


def _stack_heads(a, lo_lane):
    zero = jnp.zeros_like(a)
    return jnp.concatenate([jnp.where(lo_lane, a, zero), jnp.where(lo_lane, zero, a)], axis=0)


def _mod_kernel(cc_ref, w_ref, b_ref, o_ref):
    s = jax.nn.silu(cc_ref[...]).astype(BF16)
    o_ref[...] = _mm(s, w_ref[...].astype(BF16)) + b_ref[...]


def _run_mod(cc, w_ada, b_ada):
    nblk = (N_MOD * D_MODEL) // D_MODEL
    return pl.pallas_call(
        _mod_kernel,
        grid=(DEPTH, nblk),
        in_specs=[
            pl.BlockSpec((MOD_ROWS, D_MODEL), lambda l, j: (0, 0)),
            pl.BlockSpec((None, D_MODEL, D_MODEL), lambda l, j: (l, 0, j)),
            pl.BlockSpec((None, 1, D_MODEL), lambda l, j: (l, 0, j)),
        ],
        out_specs=pl.BlockSpec((None, MOD_ROWS, D_MODEL), lambda l, j: (l, 0, j)),
        out_shape=jax.ShapeDtypeStruct((DEPTH, MOD_ROWS, N_MOD * D_MODEL), F32),
        compiler_params=pltpu.CompilerParams(dimension_semantics=("arbitrary", "arbitrary")),
        name="ada_modulation",
    )(cc, w_ada, b_ada)


def _mix_in_kernel(x_ref, mod_ref, win_ref, convw_ref, glng_ref, glnb_ref, ws01_ref, ws23_ref, bs_ref,
                   decf_ref, decb_ref, sb0_ref,
                   ycg_ref, qkv_ref, g_ref, sb_ref, *rest, tile, conv_row, emit_states):
    if emit_states:
        sf_out_ref, sb_out_ref, sb_scr = rest
    else:
        (sb_scr,) = rest
    n_chunk = tile // CHUNK
    t = pl.program_id(1)

    @pl.when(t == 0)
    def _():
        sb_scr[...] = sb0_ref[...]

    sh_m = mod_ref[:, 0 * D_MODEL:1 * D_MODEL]
    sc_m = mod_ref[:, 1 * D_MODEL:2 * D_MODEL]
    h = (x_ref[...] * (1.0 + sc_m) + sh_m).astype(BF16)

    proj = _mm(h, win_ref[...])
    xin = proj[:, C_CONV:C_CONV + CONV_DIM]
    gate_b = proj[:, C_CONV + CONV_DIM:C_CONV + 2 * CONV_DIM]
    gate_c = proj[:, C_CONV + 2 * CONV_DIM:C_CONV + 3 * CONV_DIM]
    z = gate_c * xin
    pos_in_row = lax.broadcasted_iota(jnp.int32, (tile, 1), 0) & (conv_row - 1)
    z_prev = jnp.where(pos_in_row == 0, 0.0, pltpu.roll(z, 1, 0))
    z_next = jnp.where(pos_in_row == conv_row - 1, 0.0, pltpu.roll(z, tile - 1, 0))
    y_conv = gate_b * (convw_ref[0:1, :] * z_prev + convw_ref[1:2, :] * z + convw_ref[2:3, :] * z_next)
    ycg_ref[:, 0:CONV_DIM] = y_conv.astype(BF16)

    u_act = jax.nn.gelu(proj[:, C_GMLP:C_GMLP + GMLP_DIM])
    v_ln = _layer_norm(jax.nn.gelu(proj[:, C_GMLP + GMLP_DIM:C_GMLP + 2 * GMLP_DIM]),
                       glng_ref[...], glnb_ref[...]).astype(BF16)
    lane = lax.broadcasted_iota(jnp.int32, (1, GMLP_DIM), 1)
    head_w = GMLP_DIM // GMLP_HEADS
    in_head = [(lane >= hh * head_w) & (lane < (hh + 1) * head_w) for hh in range(GMLP_HEADS)]
    zero_bf = jnp.zeros((CHUNK, GMLP_DIM), BF16)
    for c in range(n_chunk):
        rows = slice(c * CHUNK, (c + 1) * CHUNK)
        vc = v_ln[rows, :]
        by_head = [jnp.where(in_head[hh], vc, zero_bf) for hh in range(GMLP_HEADS)]
        mixed = (_mm(ws01_ref[...], jnp.concatenate(by_head[0:2], axis=0))
                 + _mm(ws23_ref[...], jnp.concatenate(by_head[2:4], axis=0)) + bs_ref[...])
        ycg_ref[rows, CONV_DIM:CONV_DIM + GMLP_DIM] = (u_act[rows, :] * mixed).astype(BF16)

    qkv_ref[:, 0:RET_DIM] = (proj[:, C_Q:C_Q + RET_DIM] * RET_SCALE).astype(BF16)
    k = proj[:, C_K:C_K + RET_DIM]
    qkv_ref[:, RET_DIM:2 * RET_DIM] = k.astype(BF16)
    v_bf = proj[:, C_V:C_V + RET_DIM].astype(BF16)
    qkv_ref[:, 2 * RET_DIM:3 * RET_DIM] = v_bf
    g_ref[...] = proj[:, C_G:C_G + RET_DIM]

    _, block_diag = _pair_masks()
    pos = lax.broadcasted_iota(jnp.int32, (CHUNK, RET_DIM), 0).astype(F32)
    lg_b = -jnp.exp(decb_ref[...])
    w_b = jnp.exp(lg_b * pos)
    decay_chunk_b = jnp.exp(lg_b * float(CHUNK))
    if emit_states:
        lg_f = -jnp.exp(decf_ref[...])
        w_f = jnp.exp(lg_f * (float(CHUNK - 1) - pos))
        decay_chunk_f = jnp.exp(lg_f * float(CHUNK))
    for c in reversed(range(n_chunk)):
        rows = slice(c * CHUNK, (c + 1) * CHUNK)
        for p in range(N_PAIR):
            cols = slice(p * PAIR_W, (p + 1) * PAIR_W)
            prow = slice(p * PAIR_W, (p + 1) * PAIR_W)
            s_old = sb_scr[prow, :]
            sb_ref[c, prow, :] = s_old.astype(BF16)
            kw = (k[rows, cols] * w_b[:, cols]).astype(BF16)
            upd = _mm_tn(kw, v_bf[rows, cols])
            sb_scr[prow, :] = s_old * decay_chunk_b[:, cols] + jnp.where(block_diag, upd, 0.0)

    if emit_states:
        sb_out_ref[...] = sb_scr[...]
        for p in range(N_PAIR):
            cols = slice(p * PAIR_W, (p + 1) * PAIR_W)
            s_f = jnp.zeros((PAIR_W, PAIR_W), F32)
            for c in range(n_chunk):
                rows = slice(c * CHUNK, (c + 1) * CHUNK)
                kw = (k[rows, cols] * w_f[:, cols]).astype(BF16)
                upd = _mm_tn(kw, v_bf[rows, cols])
                s_f = s_f * decay_chunk_f[:, cols] + jnp.where(block_diag, upd, 0.0)
            sf_out_ref[cols, :] = s_f


def _run_mix_in(x, mod_l, win, convw, glng, glnb, ws01, ws23, bs_tile, decf, decb, sb0, *,
                tile, conv_row, mod_row, emit_states):
    bsz, n, _ = x.shape
    nt = n // tile
    n_chunk = tile // CHUNK
    if emit_states:
        assert nt == 1
    rev = lambda b, t: (b, nt - 1 - t, 0)
    const2 = lambda b, t: (0, 0)
    mod_map = (lambda b, t: (b, 0, 0)) if mod_row is None else (lambda b, t: (mod_row, 0, 0))
    state_map = lambda b, t: (b, 0, 0)
    out_shape = [
        jax.ShapeDtypeStruct((bsz, n, CONV_DIM + GMLP_DIM), BF16),
        jax.ShapeDtypeStruct((bsz, n, 3 * RET_DIM), BF16),
        jax.ShapeDtypeStruct((bsz, n, RET_DIM), F32),
        jax.ShapeDtypeStruct((bsz, n // CHUNK, RET_DIM, PAIR_W), BF16),
    ]
    out_specs = [
        pl.BlockSpec((None, tile, CONV_DIM + GMLP_DIM), rev),
        pl.BlockSpec((None, tile, 3 * RET_DIM), rev),
        pl.BlockSpec((None, tile, RET_DIM), rev),
        pl.BlockSpec((None, n_chunk, RET_DIM, PAIR_W), lambda b, t: (b, nt - 1 - t, 0, 0)),
    ]
    if emit_states:
        out_shape += [jax.ShapeDtypeStruct((bsz, RET_DIM, PAIR_W), F32)] * 2
        out_specs += [pl.BlockSpec((None, RET_DIM, PAIR_W), state_map)] * 2
    return pl.pallas_call(
        functools.partial(_mix_in_kernel, tile=tile, conv_row=conv_row, emit_states=emit_states),
        grid=(bsz, nt),
        in_specs=[
            pl.BlockSpec((None, tile, D_MODEL), rev),
            pl.BlockSpec((None, 1, N_MOD * D_MODEL), mod_map),
            pl.BlockSpec((D_MODEL, D_IN), const2),
            pl.BlockSpec((3, CONV_DIM), const2),
            pl.BlockSpec((1, GMLP_DIM), const2),
            pl.BlockSpec((1, GMLP_DIM), const2),
            pl.BlockSpec((CHUNK, 2 * CHUNK), const2),
            pl.BlockSpec((CHUNK, 2 * CHUNK), const2),
            pl.BlockSpec((CHUNK, GMLP_DIM), const2),
            pl.BlockSpec((1, RET_DIM), const2),
            pl.BlockSpec((1, RET_DIM), const2),
            pl.BlockSpec((None, RET_DIM, PAIR_W), state_map),
        ],
        out_specs=out_specs,
        out_shape=out_shape,
        scratch_shapes=[pltpu.VMEM((RET_DIM, PAIR_W), F32)],
        compiler_params=pltpu.CompilerParams(
            dimension_semantics=("arbitrary", "arbitrary"), vmem_limit_bytes=VMEM_LIMIT_BYTES),
        name="mix_in_ctx" if emit_states else "mix_in",
    )(x, mod_l, win, convw, glng, glnb, ws01, ws23, bs_tile, decf, decb, sb0)


def _mix_out_kernel(x_ref, ycg_ref, qkv_ref, g_ref, sb_ref, sf0_ref, mod_ref, lng_ref, lnb_ref,
                    wout_ref, wff1_ref, wff2_ref, decf_ref, decb_ref, decf2_ref, decb2_ref,
                    o_ref, sf_scr, d2_scr, wq_scr, y_scr, hid_scr, *, tile, sub):
    n_chunk = tile // CHUNK
    b = pl.program_id(0)
    t = pl.program_id(1)

    lg_f = -jnp.exp(decf_ref[...])
    lg_b = -jnp.exp(decb_ref[...])

    @pl.when((b == 0) & (t == 0))
    def _():
        pos = lax.broadcasted_iota(jnp.int32, (CHUNK, RET_DIM), 0).astype(F32)
        wq_scr[0] = jnp.exp(lg_f * (pos + 1.0))
        wq_scr[1] = jnp.exp(lg_b * (float(CHUNK) - pos))
        wq_scr[2] = jnp.exp(lg_f * (float(CHUNK - 1) - pos))
        i = lax.broadcasted_iota(jnp.int32, (CHUNK, 2 * CHUNK), 0)
        j = lax.broadcasted_iota(jnp.int32, (CHUNK, 2 * CHUNK), 1) & (CHUNK - 1)
        rel = i - j
        causal = rel >= 0
        relf = rel.astype(F32)
        for p in range(N_PAIR):
            lf = -jnp.exp(decf2_ref[p])
            lb = -jnp.exp(decb2_ref[p])
            d2_scr[p] = jnp.where(causal, jnp.exp(lf * jnp.where(causal, relf, 0.0)),
                                  jnp.exp(lb * jnp.where(causal, 0.0, -relf)))

    @pl.when(t == 0)
    def _():
        sf_scr[...] = sf0_ref[...]

    lo_lane, block_diag = _pair_masks()
    decay_chunk_f = jnp.exp(lg_f * float(CHUNK))

    inv_d = 1.0 / RET_HEAD_DIM

    def retention(c):
        rows = slice(c * CHUNK, (c + 1) * CHUNK)
        for p in range(N_PAIR):
            cols = slice(p * PAIR_W, (p + 1) * PAIR_W)
            qp = qkv_ref[rows, p * PAIR_W:(p + 1) * PAIR_W]
            kp = qkv_ref[rows, RET_DIM + p * PAIR_W:RET_DIM + (p + 1) * PAIR_W]
            vp = qkv_ref[rows, 2 * RET_DIM + p * PAIR_W:2 * RET_DIM + (p + 1) * PAIR_W]
            scores = _mm_nt(qp, _stack_heads(kp, lo_lane))
            probs = (scores * d2_scr[p]).astype(BF16)
            o = _mm(probs, _stack_heads(vp, lo_lane))
            s_f = sf_scr[cols, :]
            o = o + wq_scr[0, :, cols] * _mm(qp, s_f.astype(BF16))
            o = o + wq_scr[1, :, cols] * _mm(qp, sb_ref[c, cols, :])
            kw = (kp.astype(F32) * wq_scr[2, :, cols]).astype(BF16)
            sf_scr[cols, :] = s_f * decay_chunk_f[:, cols] + jnp.where(block_diag, _mm_tn(kw, vp), 0.0)
            mu = jnp.where(lo_lane,
                           jnp.sum(jnp.where(lo_lane, o, 0.0), axis=-1, keepdims=True),
                           jnp.sum(jnp.where(lo_lane, 0.0, o), axis=-1, keepdims=True)) * inv_d
            d = o - mu
            dd = d * d
            var = jnp.where(lo_lane,
                            jnp.sum(jnp.where(lo_lane, dd, 0.0), axis=-1, keepdims=True),
                            jnp.sum(jnp.where(lo_lane, 0.0, dd), axis=-1, keepdims=True)) * inv_d
            o_n = d * lax.rsqrt(var + LN_EPS)
            y_scr[rows, RET_DIM + p * PAIR_W:RET_DIM + (p + 1) * PAIR_W] = (
                jax.nn.silu(g_ref[rows, cols]) * o_n).astype(BF16)

    gt_m = mod_ref[:, 2 * D_MODEL:3 * D_MODEL]
    sh_f = mod_ref[:, 3 * D_MODEL:4 * D_MODEL]
    sc_f = mod_ref[:, 4 * D_MODEL:5 * D_MODEL]
    gt_f = mod_ref[:, 5 * D_MODEL:6 * D_MODEL]

    blocks = [slice(s * sub, (s + 1) * sub) for s in range(tile // sub)]
    y_scr[:, 0:CONV_DIM + GMLP_DIM] = ycg_ref[...]
    for c in range(n_chunk):
        retention(c)
    z = [_mm(y_scr[r, :], wout_ref[...]) for r in blocks]
    for r, z_r in zip(blocks, z):
        x1 = _layer_norm(ALPHA * x_ref[r, :] + gt_m * z_r, lng_ref[0:1, :], lnb_ref[0:1, :])
        o_ref[r, :] = x1
        hf = (x1 * (1.0 + sc_f) + sh_f).astype(BF16)
        for nb in range(D_FF // FF_BLOCK):
            cols = slice(nb * FF_BLOCK, (nb + 1) * FF_BLOCK)
            a = jnp.maximum(_mm(hf, wff1_ref[:, cols]), 0.0)
            hid_scr[r, cols] = (a * a).astype(BF16)
    y_ff = [_mm(hid_scr[r, :], wff2_ref[...]) for r in blocks]
    for r, y_r in zip(blocks, y_ff):
        o_ref[r, :] = _layer_norm(ALPHA * o_ref[r, :] + gt_f * y_r, lng_ref[1:2, :], lnb_ref[1:2, :])


def _run_mix_out(x, ycg, qkv, g, sb, sf0, mod_l, lng, lnb, wout, wff1, wff2, decf, decb, decf2, decb2, *,
                 tile, mod_row):
    bsz, n, _ = x.shape
    nt = n // tile
    n_chunk = tile // CHUNK
    fwd = lambda b, t: (b, t, 0)
    const2 = lambda b, t: (0, 0)
    const3 = lambda b, t: (0, 0, 0)
    mod_map = (lambda b, t: (b, 0, 0)) if mod_row is None else (lambda b, t: (mod_row, 0, 0))
    resident = dict(pipeline_mode=pl.Buffered(1))
    return pl.pallas_call(
        functools.partial(_mix_out_kernel, tile=tile, sub=min(tile, SUB_ROWS)),
        grid=(bsz, nt),
        in_specs=[
            pl.BlockSpec((None, tile, D_MODEL), fwd),
            pl.BlockSpec((None, tile, CONV_DIM + GMLP_DIM), fwd),
            pl.BlockSpec((None, tile, 3 * RET_DIM), fwd),
            pl.BlockSpec((None, tile, RET_DIM), fwd),
            pl.BlockSpec((None, n_chunk, RET_DIM, PAIR_W), lambda b, t: (b, t, 0, 0)),
            pl.BlockSpec((None, RET_DIM, PAIR_W), lambda b, t: (b, 0, 0)),
            pl.BlockSpec((None, 1, N_MOD * D_MODEL), mod_map),
            pl.BlockSpec((2, D_MODEL), const2),
            pl.BlockSpec((2, D_MODEL), const2),
            pl.BlockSpec((D_MODEL, D_MODEL), const2, **resident),
            pl.BlockSpec((D_MODEL, D_FF), const2, **resident),
            pl.BlockSpec((D_FF, D_MODEL), const2, **resident),
            pl.BlockSpec((1, RET_DIM), const2),
            pl.BlockSpec((1, RET_DIM), const2),
            pl.BlockSpec((N_PAIR, 1, 2 * CHUNK), const3),
            pl.BlockSpec((N_PAIR, 1, 2 * CHUNK), const3),
        ],
        out_specs=pl.BlockSpec((None, tile, D_MODEL), fwd),
        out_shape=jax.ShapeDtypeStruct((bsz, n, D_MODEL), F32),
        scratch_shapes=[
            pltpu.VMEM((RET_DIM, PAIR_W), F32),
            pltpu.VMEM((N_PAIR, CHUNK, 2 * CHUNK), F32),
            pltpu.VMEM((3, CHUNK, RET_DIM), F32),
            pltpu.VMEM((tile, D_MODEL), BF16),
            pltpu.VMEM((tile, D_FF), BF16),
        ],
        compiler_params=pltpu.CompilerParams(
            dimension_semantics=("arbitrary", "arbitrary"), vmem_limit_bytes=VMEM_LIMIT_BYTES),
        name="mix_out",
    )(x, ycg, qkv, g, sb, sf0, mod_l, lng, lnb, wout, wff1, wff2, decf, decb, decf2, decb2)


def kernel(x, c, ctx, c_ctx, w_ada, b_ada, w_in, conv_w, gmlp_ln_g, gmlp_ln_b, gmlp_ws, gmlp_bs,
           ret_decay_fwd, ret_decay_bwd, w_out, w_ff1, w_ff2, ln_g, ln_b):
    bsz, n_lat, _ = x.shape
    ctx_len = ctx.shape[1]
    lat_tile = 512
    out_tile = 512
    assert n_lat % lat_tile == 0 and lat_tile % CHUNK == 0 and ctx_len % CHUNK == 0
    assert n_lat % out_tile == 0 and out_tile % SUB_ROWS == 0
    assert ctx_len & (ctx_len - 1) == 0 and GRID_W & (GRID_W - 1) == 0

    win_b = w_in.astype(BF16)
    wout_b = w_out.astype(BF16)
    wff1_b = w_ff1.astype(BF16)
    wff2_b = w_ff2.astype(BF16)

    cc = jnp.concatenate([c, c_ctx[None, :], jnp.zeros((MOD_ROWS - bsz - 1, D_MODEL), F32)], axis=0)
    mod = _run_mod(cc, w_ada, b_ada.reshape(DEPTH, 1, N_MOD * D_MODEL))
    mod = mod.reshape(DEPTH, MOD_ROWS, 1, N_MOD * D_MODEL)

    zero_state = jnp.zeros((bsz, RET_DIM, PAIR_W), F32)
    xc = ctx
    for l in range(DEPTH):
        last = l == DEPTH - 1
        ws = gmlp_ws[l].astype(BF16)
        ws01 = jnp.concatenate([ws[0], ws[1]], axis=1)
        ws23 = jnp.concatenate([ws[2], ws[3]], axis=1)
        bs_tile = jnp.repeat(gmlp_bs[l].T, GMLP_DIM // GMLP_HEADS, axis=1)
        decf = jnp.repeat(ret_decay_fwd[l].astype(F32), RET_HEAD_DIM)[None, :]
        decb = jnp.repeat(ret_decay_bwd[l].astype(F32), RET_HEAD_DIM)[None, :]
        decf2 = jnp.repeat(ret_decay_fwd[l].astype(F32), CHUNK).reshape(N_PAIR, 1, 2 * CHUNK)
        decb2 = jnp.repeat(ret_decay_bwd[l].astype(F32), CHUNK).reshape(N_PAIR, 1, 2 * CHUNK)
        mix_consts = (mod[l], win_b[l], conv_w[l], gmlp_ln_g[l][None, :], gmlp_ln_b[l][None, :],
                      ws01, ws23, bs_tile, decf, decb)
        out_consts = (mod[l], ln_g[l], ln_b[l], wout_b[l], wff1_b[l], wff2_b[l], decf, decb, decf2, decb2)

        ycg_c, qkv_c, g_c, sb_c, sf_ctx, sb_ctx = _run_mix_in(
            xc, *mix_consts, zero_state, tile=ctx_len, conv_row=ctx_len, mod_row=CTX_MOD_ROW, emit_states=True)
        ycg, qkv, g, sb = _run_mix_in(
            x, *mix_consts, sb_ctx, tile=lat_tile, conv_row=GRID_W, mod_row=None, emit_states=False)
        x = _run_mix_out(x, ycg, qkv, g, sb, sf_ctx, *out_consts, tile=out_tile, mod_row=None)
        if not last:
            xc = _run_mix_out(xc, ycg_c, qkv_c, g_c, sb_c, zero_state, *out_consts,
                              tile=ctx_len, mod_row=CTX_MOD_ROW)
    return x
```

```python
import functools

import jax
import jax.numpy as jnp
from jax import lax
from jax.experimental import pallas as pl
from jax.experimental.pallas import tpu as pltpu

D_MODEL = 1024
DEPTH = 2
GRID_W = 64
CONV_DIM = 256
GMLP_DIM = 256
GMLP_HEADS = 4
CHUNK = 128
RET_DIM = 512
RET_HEADS = 8
RET_HEAD_DIM = 64
RET_SCALE = RET_HEAD_DIM ** -0.5
D_FF = 4 * D_MODEL
N_MOD = 6
LN_EPS = 1e-5
ALPHA = (2 * DEPTH) ** 0.25
D_IN = 3 * CONV_DIM + 2 * GMLP_DIM + 4 * RET_DIM

C_CONV = 0
C_GMLP = 3 * CONV_DIM
C_Q = C_GMLP + 2 * GMLP_DIM
C_K = C_Q + RET_DIM
C_V = C_K + RET_DIM
C_G = C_V + RET_DIM

N_PAIR = RET_HEADS // 2
PAIR_W = 2 * RET_HEAD_DIM
MOD_ROWS = 16
CTX_MOD_ROW = 8
FF_BLOCK = 1024
SUB_ROWS = 256

VMEM_LIMIT_BYTES = 56 * 1024 * 1024

F32 = jnp.float32
BF16 = jnp.bfloat16


def _layer_norm(v, g, b):
    mu = jnp.mean(v, axis=-1, keepdims=True)
    d = v - mu
    var = jnp.mean(d * d, axis=-1, keepdims=True)
    return d * lax.rsqrt(var + LN_EPS) * g + b


def _mm(a, b):
    return jnp.dot(a, b, preferred_element_type=F32)


def _mm_nt(a, b):
    return lax.dot_general(a, b, (((1,), (1,)), ((), ())), preferred_element_type=F32)


def _mm_tn(a, b):
    return lax.dot_general(a, b, (((0,), (0,)), ((), ())), preferred_element_type=F32)


def _pair_masks():
    lane = lax.broadcasted_iota(jnp.int32, (1, PAIR_W), 1)
    row = lax.broadcasted_iota(jnp.int32, (PAIR_W, 1), 0)
    lo_lane = lane < RET_HEAD_DIM
    block_diag = (row < RET_HEAD_DIM) == lo_lane
    return lo_lane, block_diag


def _stack_heads(a, lo_lane):
    zero = jnp.zeros_like(a)
    return jnp.concatenate([jnp.where(lo_lane, a, zero), jnp.where(lo_lane, zero, a)], axis=0)


def _mod_kernel(cc_ref, w_ref, b_ref, o_ref):
    s = jax.nn.silu(cc_ref[...]).astype(BF16)
    o_ref[...] = _mm(s, w_ref[...].astype(BF16)) + b_ref[...]


def _run_mod(cc, w_ada, b_ada):
    nblk = (N_MOD * D_MODEL) // D_MODEL
    return pl.pallas_call(
        _mod_kernel,
        grid=(DEPTH, nblk),
        in_specs=[
            pl.BlockSpec((MOD_ROWS, D_MODEL), lambda l, j: (0, 0)),
            pl.BlockSpec((None, D_MODEL, D_MODEL), lambda l, j: (l, 0, j)),
            pl.BlockSpec((None, 1, D_MODEL), lambda l, j: (l, 0, j)),
        ],
        out_specs=pl.BlockSpec((None, MOD_ROWS, D_MODEL), lambda l, j: (l, 0, j)),
        out_shape=jax.ShapeDtypeStruct((DEPTH, MOD_ROWS, N_MOD * D_MODEL), F32),
        compiler_params=pltpu.CompilerParams(dimension_semantics=("arbitrary", "arbitrary")),
        name="ada_modulation",
    )(cc, w_ada, b_ada)


def _mix_in_kernel(x_ref, mod_ref, win_ref, convw_ref, glng_ref, glnb_ref, ws01_ref, ws23_ref, bs_ref,
                   decf_ref, decb_ref, sb0_ref, *rest, tile, conv_row, emit_states, states_only):
    if states_only:
        sf_out_ref, sb_out_ref, sb_scr = rest
    elif emit_states:
        ycg_ref, qkv_ref, g_ref, sb_ref, sf_out_ref, sb_out_ref, sb_scr = rest
    else:
        ycg_ref, qkv_ref, g_ref, sb_ref, sb_scr = rest
    n_chunk = tile // CHUNK
    t = pl.program_id(1)

    @pl.when(t == 0)
    def _():
        sb_scr[...] = sb0_ref[...]

    sh_m = mod_ref[:, 0 * D_MODEL:1 * D_MODEL]
    sc_m = mod_ref[:, 1 * D_MODEL:2 * D_MODEL]
    h = (x_ref[...] * (1.0 + sc_m) + sh_m).astype(BF16)

    if states_only:
        kv = _mm(h, win_ref[:, C_K:C_K + 2 * RET_DIM])
        k = kv[:, 0:RET_DIM]
        v_bf = kv[:, RET_DIM:2 * RET_DIM].astype(BF16)
    else:
        proj = _mm(h, win_ref[...])

        xin = proj[:, C_CONV:C_CONV + CONV_DIM]
        gate_b = proj[:, C_CONV + CONV_DIM:C_CONV + 2 * CONV_DIM]
        gate_c = proj[:, C_CONV + 2 * CONV_DIM:C_CONV + 3 * CONV_DIM]
        z = gate_c * xin
        pos_in_row = lax.broadcasted_iota(jnp.int32, (tile, 1), 0) & (conv_row - 1)
        z_prev = jnp.where(pos_in_row == 0, 0.0, pltpu.roll(z, 1, 0))
        z_next = jnp.where(pos_in_row == conv_row - 1, 0.0, pltpu.roll(z, tile - 1, 0))
        y_conv = gate_b * (convw_ref[0:1, :] * z_prev + convw_ref[1:2, :] * z + convw_ref[2:3, :] * z_next)
        ycg_ref[:, 0:CONV_DIM] = y_conv.astype(BF16)

        u_act = jax.nn.gelu(proj[:, C_GMLP:C_GMLP + GMLP_DIM])
        v_ln = _layer_norm(jax.nn.gelu(proj[:, C_GMLP + GMLP_DIM:C_GMLP + 2 * GMLP_DIM]),
                           glng_ref[...], glnb_ref[...]).astype(BF16)
        lane = lax.broadcasted_iota(jnp.int32, (1, GMLP_DIM), 1)
        head_w = GMLP_DIM // GMLP_HEADS
        in_head = [(lane >= hh * head_w) & (lane < (hh + 1) * head_w) for hh in range(GMLP_HEADS)]
        zero_bf = jnp.zeros((CHUNK, GMLP_DIM), BF16)
        for c in range(n_chunk):
            rows = slice(c * CHUNK, (c + 1) * CHUNK)
            vc = v_ln[rows, :]
            by_head = [jnp.where(in_head[hh], vc, zero_bf) for hh in range(GMLP_HEADS)]
            mixed = (_mm(ws01_ref[...], jnp.concatenate(by_head[0:2], axis=0))
                     + _mm(ws23_ref[...], jnp.concatenate(by_head[2:4], axis=0)) + bs_ref[...])
            ycg_ref[rows, CONV_DIM:CONV_DIM + GMLP_DIM] = (u_act[rows, :] * mixed).astype(BF16)

        qkv_ref[:, 0:RET_DIM] = (proj[:, C_Q:C_Q + RET_DIM] * RET_SCALE).astype(BF16)
        k = proj[:, C_K:C_K + RET_DIM]
        qkv_ref[:, RET_DIM:2 * RET_DIM] = k.astype(BF16)
        v_bf = proj[:, C_V:C_V + RET_DIM].astype(BF16)
        qkv_ref[:, 2 * RET_DIM:3 * RET_DIM] = v_bf
        g_ref[...] = proj[:, C_G:C_G + RET_DIM]

    _, block_diag = _pair_masks()
    pos = lax.broadcasted_iota(jnp.int32, (CHUNK, RET_DIM), 0).astype(F32)
    lg_b = -jnp.exp(decb_ref[...])
    w_b = jnp.exp(lg_b * pos)
    decay_chunk_b = jnp.exp(lg_b * float(CHUNK))
    if emit_states:
        lg_f = -jnp.exp(decf_ref[...])
        w_f = jnp.exp(lg_f * (float(CHUNK - 1) - pos))
        decay_chunk_f = jnp.exp(lg_f * float(CHUNK))
    for c in reversed(range(n_chunk)):
        rows = slice(c * CHUNK, (c + 1) * CHUNK)
        for p in range(N_PAIR):
            cols = slice(p * PAIR_W, (p + 1) * PAIR_W)
            s_old = sb_scr[cols, :]
            if not states_only:
                sb_ref[c, cols, :] = s_old.astype(BF16)
            kw = (k[rows, cols] * w_b[:, cols]).astype(BF16)
            upd = _mm_tn(kw, v_bf[rows, cols])
            sb_scr[cols, :] = s_old * decay_chunk_b[:, cols] + jnp.where(block_diag, upd, 0.0)

    if emit_states:
        sb_out_ref[...] = sb_scr[...]
        for p in range(N_PAIR):
            cols = slice(p * PAIR_W, (p + 1) * PAIR_W)
            s_f = jnp.zeros((PAIR_W, PAIR_W), F32)
            for c in range(n_chunk):
                rows = slice(c * CHUNK, (c + 1) * CHUNK)
                kw = (k[rows, cols] * w_f[:, cols]).astype(BF16)
                upd = _mm_tn(kw, v_bf[rows, cols])
                s_f = s_f * decay_chunk_f[:, cols] + jnp.where(block_diag, upd, 0.0)
            sf_out_ref[cols, :] = s_f


def _run_mix_in(x, layer, mod, win, convw, glng, glnb, ws01, ws23, bs_tile, decf, decb, sb0, *,
                tile, conv_row, mod_row, emit_states, states_only=False):
    bsz, n, _ = x.shape
    nt = n // tile
    n_chunk = tile // CHUNK
    if emit_states:
        assert nt == 1
    assert emit_states or not states_only
    rev = lambda b, t: (b, nt - 1 - t, 0)
    per_layer = lambda b, t: (layer, 0, 0)
    mod_map = (lambda b, t: (layer, b, 0, 0)) if mod_row is None else (lambda b, t: (layer, mod_row, 0, 0))
    state_map = lambda b, t: (b, 0, 0)
    out_shape, out_specs = [], []
    if not states_only:
        out_shape += [
            jax.ShapeDtypeStruct((bsz, n, CONV_DIM + GMLP_DIM), BF16),
            jax.ShapeDtypeStruct((bsz, n, 3 * RET_DIM), BF16),
            jax.ShapeDtypeStruct((bsz, n, RET_DIM), F32),
            jax.ShapeDtypeStruct((bsz, n // CHUNK, RET_DIM, PAIR_W), BF16),
        ]
        out_specs += [
            pl.BlockSpec((None, tile, CONV_DIM + GMLP_DIM), rev),
            pl.BlockSpec((None, tile, 3 * RET_DIM), rev),
            pl.BlockSpec((None, tile, RET_DIM), rev),
            pl.BlockSpec((None, n_chunk, RET_DIM, PAIR_W), lambda b, t: (b, nt - 1 - t, 0, 0)),
        ]
    if emit_states:
        out_shape += [jax.ShapeDtypeStruct((bsz, RET_DIM, PAIR_W), F32)] * 2
        out_specs += [pl.BlockSpec((None, RET_DIM, PAIR_W), state_map)] * 2
    return pl.pallas_call(
        functools.partial(_mix_in_kernel, tile=tile, conv_row=conv_row, emit_states=emit_states,
                          states_only=states_only),
        grid=(bsz, nt),
        in_specs=[
            pl.BlockSpec((None, tile, D_MODEL), rev),
            pl.BlockSpec((None, None, 1, N_MOD * D_MODEL), mod_map),
            pl.BlockSpec((None, D_MODEL, D_IN), per_layer),
            pl.BlockSpec((None, 3, CONV_DIM), per_layer),
            pl.BlockSpec((None, 1, GMLP_DIM), per_layer),
            pl.BlockSpec((None, 1, GMLP_DIM), per_layer),
            pl.BlockSpec((None, CHUNK, 2 * CHUNK), per_layer),
            pl.BlockSpec((None, CHUNK, 2 * CHUNK), per_layer),
            pl.BlockSpec((None, CHUNK, GMLP_DIM), per_layer),
            pl.BlockSpec((None, 1, RET_DIM), per_layer),
            pl.BlockSpec((None, 1, RET_DIM), per_layer),
            pl.BlockSpec((None, RET_DIM, PAIR_W), state_map),
        ],
        out_specs=out_specs,
        out_shape=out_shape,
        scratch_shapes=[pltpu.VMEM((RET_DIM, PAIR_W), F32)],
        compiler_params=pltpu.CompilerParams(
            dimension_semantics=("arbitrary", "arbitrary"), vmem_limit_bytes=VMEM_LIMIT_BYTES),
        name="ctx_states" if states_only else ("mix_in_ctx" if emit_states else "mix_in"),
    )(x, mod, win, convw, glng, glnb, ws01, ws23, bs_tile, decf, decb, sb0)


def _mix_out_kernel(x_ref, ycg_ref, qkv_ref, g_ref, sb_ref, sf0_ref, mod_ref, lng_ref, lnb_ref,
                    wout_ref, wff1_ref, wff2_ref, decf_ref, decb_ref, decf2_ref, decb2_ref,
                    o_ref, sf_scr, d2_scr, wq_scr, y_scr, hid_scr, *, tile, sub, tiles_per_seq, n_tiles):
    n_chunk = tile // CHUNK
    s = pl.program_id(0)
    t_ret = lax.rem(jnp.minimum(s, n_tiles - 1), tiles_per_seq)

    lg_f = -jnp.exp(decf_ref[...])
    lg_b = -jnp.exp(decb_ref[...])

    @pl.when(s == 0)
    def _():
        y_scr[...] = jnp.zeros_like(y_scr)
        pos = lax.broadcasted_iota(jnp.int32, (CHUNK, RET_DIM), 0).astype(F32)
        wq_scr[0] = jnp.exp(lg_f * (pos + 1.0))
        wq_scr[1] = jnp.exp(lg_b * (float(CHUNK) - pos))
        wq_scr[2] = jnp.exp(lg_f * (float(CHUNK - 1) - pos))
        i = lax.broadcasted_iota(jnp.int32, (CHUNK, 2 * CHUNK), 0)
        j = lax.broadcasted_iota(jnp.int32, (CHUNK, 2 * CHUNK), 1) & (CHUNK - 1)
        rel = i - j
        causal = rel >= 0
        relf = rel.astype(F32)
        for p in range(N_PAIR):
            lf = -jnp.exp(decf2_ref[p])
            lb = -jnp.exp(decb2_ref[p])
            d2_scr[p] = jnp.where(causal, jnp.exp(lf * jnp.where(causal, relf, 0.0)),
                                  jnp.exp(lb * jnp.where(causal, 0.0, -relf)))

    @pl.when(t_ret == 0)
    def _():
        sf_scr[...] = sf0_ref[...]

    lo_lane, block_diag = _pair_masks()
    decay_chunk_f = jnp.exp(lg_f * float(CHUNK))
    inv_d = 1.0 / RET_HEAD_DIM

    gt_m = mod_ref[:, 2 * D_MODEL:3 * D_MODEL]
    sh_f = mod_ref[:, 3 * D_MODEL:4 * D_MODEL]
    sc_f = mod_ref[:, 4 * D_MODEL:5 * D_MODEL]
    gt_f = mod_ref[:, 5 * D_MODEL:6 * D_MODEL]

    units = [(c, p) for c in range(n_chunk) for p in range(N_PAIR)]
    rows = lambda c: slice(c * CHUNK, (c + 1) * CHUNK)
    cols = lambda p: slice(p * PAIR_W, (p + 1) * PAIR_W)
    q_of = lambda c, p: qkv_ref[rows(c), p * PAIR_W:(p + 1) * PAIR_W]
    k_of = lambda c, p: qkv_ref[rows(c), RET_DIM + p * PAIR_W:RET_DIM + (p + 1) * PAIR_W]
    v_of = lambda c, p: qkv_ref[rows(c), 2 * RET_DIM + p * PAIR_W:2 * RET_DIM + (p + 1) * PAIR_W]
    blocks = [slice(i * sub, (i + 1) * sub) for i in range(tile // sub)]

    scores = {u: _mm_nt(q_of(*u), _stack_heads(k_of(*u), lo_lane)) for u in units}
    upd = {}
    for c, p in units:
        kw = (k_of(c, p).astype(F32) * wq_scr[2, :, cols(p)]).astype(BF16)
        upd[c, p] = _mm_tn(kw, v_of(c, p))
    cross_b = {(c, p): _mm(q_of(c, p), sb_ref[c, cols(p), :]) for c, p in units}

    z = [_mm(y_scr[r, :], wout_ref[...]) for r in blocks]

    s_in = {}
    for p in range(N_PAIR):
        s_f = sf_scr[cols(p), :]
        for c in range(n_chunk):
            s_in[c, p] = s_f
            s_f = s_f * decay_chunk_f[:, cols(p)] + jnp.where(block_diag, upd[c, p], 0.0)
        sf_scr[cols(p), :] = s_f
    o_in = {(c, p): _mm((scores[c, p] * d2_scr[p]).astype(BF16), _stack_heads(v_of(c, p), lo_lane))
            for c, p in units}
    cross_f = {u: _mm(q_of(*u), s_in[u].astype(BF16)) for u in units}

    for r, z_r in zip(blocks, z):
        x1 = _layer_norm(ALPHA * x_ref[r, :] + gt_m * z_r, lng_ref[0:1, :], lnb_ref[0:1, :])
        o_ref[r, :] = x1
        hf = (x1 * (1.0 + sc_f) + sh_f).astype(BF16)
        for nb in range(D_FF // FF_BLOCK):
            ff_cols = slice(nb * FF_BLOCK, (nb + 1) * FF_BLOCK)
            a = jnp.maximum(_mm(hf, wff1_ref[:, ff_cols]), 0.0)
            hid_scr[r, ff_cols] = (a * a).astype(BF16)

    y_scr[:, 0:CONV_DIM + GMLP_DIM] = ycg_ref[...]
    for c, p in units:
        o = o_in[c, p] + wq_scr[0, :, cols(p)] * cross_f[c, p] + wq_scr[1, :, cols(p)] * cross_b[c, p]
        mu = jnp.where(lo_lane,
                       jnp.sum(jnp.where(lo_lane, o, 0.0), axis=-1, keepdims=True),
                       jnp.sum(jnp.where(lo_lane, 0.0, o), axis=-1, keepdims=True)) * inv_d
        d = o - mu
        dd = d * d
        var = jnp.where(lo_lane,
                        jnp.sum(jnp.where(lo_lane, dd, 0.0), axis=-1, keepdims=True),
                        jnp.sum(jnp.where(lo_lane, 0.0, dd), axis=-1, keepdims=True)) * inv_d
        o_n = d * lax.rsqrt(var + LN_EPS)
        y_scr[rows(c), RET_DIM + p * PAIR_W:RET_DIM + (p + 1) * PAIR_W] = (
            jax.nn.silu(g_ref[rows(c), cols(p)]) * o_n).astype(BF16)

    y_ff = [_mm(hid_scr[r, :], wff2_ref[...]) for r in blocks]
    for r, y_r in zip(blocks, y_ff):
        o_ref[r, :] = _layer_norm(ALPHA * o_ref[r, :] + gt_f * y_r, lng_ref[1:2, :], lnb_ref[1:2, :])


def _run_mix_out(x, ycg, qkv, g, sb, sf0, layer, mod, lng, lnb, wout, wff1, wff2, decf, decb, decf2, decb2, *,
                 tile, mod_row):
    bsz, n, _ = x.shape
    nt = n // tile
    n_tiles = bsz * nt
    n_chunk = tile // CHUNK

    def ret_tile(s):
        i = jnp.minimum(s, n_tiles - 1)
        return lax.div(i, nt), lax.rem(i, nt)

    def mm_tile(s):
        i = jnp.maximum(s - 1, 0)
        return lax.div(i, nt), lax.rem(i, nt)

    ret3 = lambda s: (*ret_tile(s), 0)
    mm3 = lambda s: (*mm_tile(s), 0)
    per_layer = lambda s: (layer, 0, 0)
    per_layer4 = lambda s: (layer, 0, 0, 0)
    mod_map = ((lambda s: (layer, mm_tile(s)[0], 0, 0)) if mod_row is None
               else (lambda s: (layer, mod_row, 0, 0)))
    resident = dict(pipeline_mode=pl.Buffered(1))
    return pl.pallas_call(
        functools.partial(_mix_out_kernel, tile=tile, sub=min(tile, SUB_ROWS), tiles_per_seq=nt, n_tiles=n_tiles),
        grid=(n_tiles + 1,),
        in_specs=[
            pl.BlockSpec((None, tile, D_MODEL), mm3),
            pl.BlockSpec((None, tile, CONV_DIM + GMLP_DIM), ret3),
            pl.BlockSpec((None, tile, 3 * RET_DIM), ret3),
            pl.BlockSpec((None, tile, RET_DIM), ret3),
            pl.BlockSpec((None, n_chunk, RET_DIM, PAIR_W), lambda s: (*ret_tile(s), 0, 0)),
            pl.BlockSpec((None, RET_DIM, PAIR_W), lambda s: (ret_tile(s)[0], 0, 0)),
            pl.BlockSpec((None, None, 1, N_MOD * D_MODEL), mod_map),
            pl.BlockSpec((None, 2, D_MODEL), per_layer),
            pl.BlockSpec((None, 2, D_MODEL), per_layer),
            pl.BlockSpec((None, D_MODEL, D_MODEL), per_layer, **resident),
            pl.BlockSpec((None, D_MODEL, D_FF), per_layer, **resident),
            pl.BlockSpec((None, D_FF, D_MODEL), per_layer, **resident),
            pl.BlockSpec((None, 1, RET_DIM), per_layer),
            pl.BlockSpec((None, 1, RET_DIM), per_layer),
            pl.BlockSpec((None, N_PAIR, 1, 2 * CHUNK), per_layer4),
            pl.BlockSpec((None, N_PAIR, 1, 2 * CHUNK), per_layer4),
        ],
        out_specs=pl.BlockSpec((None, tile, D_MODEL), mm3),
        out_shape=jax.ShapeDtypeStruct((bsz, n, D_MODEL), F32),
        scratch_shapes=[
            pltpu.VMEM((RET_DIM, PAIR_W), F32),
            pltpu.VMEM((N_PAIR, CHUNK, 2 * CHUNK), F32),
            pltpu.VMEM((3, CHUNK, RET_DIM), F32),
            pltpu.VMEM((tile, D_MODEL), BF16),
            pltpu.VMEM((tile, D_FF), BF16),
        ],
        compiler_params=pltpu.CompilerParams(
            dimension_semantics=("arbitrary",), vmem_limit_bytes=VMEM_LIMIT_BYTES),
        name="mix_out",
    )(x, ycg, qkv, g, sb, sf0, mod, lng, lnb, wout, wff1, wff2, decf, decb, decf2, decb2)


def kernel(x, c, ctx, c_ctx, w_ada, b_ada, w_in, conv_w, gmlp_ln_g, gmlp_ln_b, gmlp_ws, gmlp_bs,
           ret_decay_fwd, ret_decay_bwd, w_out, w_ff1, w_ff2, ln_g, ln_b):
    bsz, n_lat, _ = x.shape
    ctx_len = ctx.shape[1]
    lat_tile = 512
    out_tile = 512
    assert n_lat % lat_tile == 0 and lat_tile % CHUNK == 0 and ctx_len % CHUNK == 0
    assert n_lat % out_tile == 0 and out_tile % SUB_ROWS == 0
    assert ctx_len & (ctx_len - 1) == 0 and GRID_W & (GRID_W - 1) == 0

    win_b = w_in.astype(BF16)
    wout_b = w_out.astype(BF16)
    wff1_b = w_ff1.astype(BF16)
    wff2_b = w_ff2.astype(BF16)

    cc = jnp.concatenate([c, c_ctx[None, :], jnp.zeros((MOD_ROWS - bsz - 1, D_MODEL), F32)], axis=0)
    mod = _run_mod(cc, w_ada, b_ada.reshape(DEPTH, 1, N_MOD * D_MODEL))
    mod = mod.reshape(DEPTH, MOD_ROWS, 1, N_MOD * D_MODEL)

    ws = gmlp_ws.astype(BF16)
    ws01 = jnp.concatenate([ws[:, 0], ws[:, 1]], axis=2)
    ws23 = jnp.concatenate([ws[:, 2], ws[:, 3]], axis=2)
    bs_tile = jnp.repeat(jnp.swapaxes(gmlp_bs, 1, 2), GMLP_DIM // GMLP_HEADS, axis=2)
    decf = jnp.repeat(ret_decay_fwd.astype(F32), RET_HEAD_DIM, axis=1)[:, None, :]
    decb = jnp.repeat(ret_decay_bwd.astype(F32), RET_HEAD_DIM, axis=1)[:, None, :]
    decf2 = jnp.repeat(ret_decay_fwd.astype(F32), CHUNK, axis=1).reshape(DEPTH, N_PAIR, 1, 2 * CHUNK)
    decb2 = jnp.repeat(ret_decay_bwd.astype(F32), CHUNK, axis=1).reshape(DEPTH, N_PAIR, 1, 2 * CHUNK)
    mix_consts = (mod, win_b, conv_w, gmlp_ln_g[:, None, :], gmlp_ln_b[:, None, :], ws01, ws23, bs_tile, decf, decb)
    out_consts = (mod, ln_g, ln_b, wout_b, wff1_b, wff2_b, decf, decb, decf2, decb2)

    zero_state = jnp.zeros((bsz, RET_DIM, PAIR_W), F32)
    xc = ctx
    for l in range(DEPTH):
        last = l == DEPTH - 1
        ctx_out = _run_mix_in(xc, l, *mix_consts, zero_state, tile=ctx_len, conv_row=ctx_len,
                              mod_row=CTX_MOD_ROW, emit_states=True, states_only=last)
        sf_ctx, sb_ctx = ctx_out[-2:]
        ycg, qkv, g, sb = _run_mix_in(x, l, *mix_consts, sb_ctx, tile=lat_tile, conv_row=GRID_W,
                                      mod_row=None, emit_states=False)
        x = _run_mix_out(x, ycg, qkv, g, sb, sf_ctx, l, *out_consts, tile=out_tile, mod_row=None)
        if not last:
            ycg_c, qkv_c, g_c, sb_c = ctx_out[:4]
            xc = _run_mix_out(xc, ycg_c, qkv_c, g_c, sb_c, zero_state, l, *out_consts,
                              tile=ctx_len, mod_row=CTX_MOD_ROW)
    return x
```

```python
import functools

import jax
import jax.numpy as jnp
from jax import lax
from jax.experimental import pallas as pl
from jax.experimental.pallas import tpu as pltpu

D_MODEL = 1024
DEPTH = 2
GRID_W = 64
CONV_DIM = 256
GMLP_DIM = 256
GMLP_HEADS = 4
CHUNK = 128
RET_DIM = 512
RET_HEADS = 8
RET_HEAD_DIM = 64
RET_SCALE = RET_HEAD_DIM ** -0.5
D_FF = 4 * D_MODEL
N_MOD = 6
LN_EPS = 1e-5
ALPHA = (2 * DEPTH) ** 0.25
D_IN = 3 * CONV_DIM + 2 * GMLP_DIM + 4 * RET_DIM

C_CONV = 0
C_GMLP = 3 * CONV_DIM
C_Q = C_GMLP + 2 * GMLP_DIM
C_K = C_Q + RET_DIM
C_V = C_K + RET_DIM
C_G = C_V + RET_DIM

N_PAIR = RET_HEADS // 2
PAIR_W = 2 * RET_HEAD_DIM
MOD_ROWS = 16
CTX_MOD_ROW = 8
FF_BLOCK = 1024
SUB_ROWS = 256

VMEM_LIMIT_BYTES = 56 * 1024 * 1024

F32 = jnp.float32
BF16 = jnp.bfloat16


def _layer_norm(v, g, b):
    mu = jnp.mean(v, axis=-1, keepdims=True)
    d = v - mu
    var = jnp.mean(d * d, axis=-1, keepdims=True)
    return d * lax.rsqrt(var + LN_EPS) * g + b


def _mm(a, b):
    return jnp.dot(a, b, preferred_element_type=F32)


def _mm_nt(a, b):
    return lax.dot_general(a, b, (((1,), (1,)), ((), ())), preferred_element_type=F32)


def _mm_tn(a, b):
    return lax.dot_general(a, b, (((0,), (0,)), ((), ())), preferred_element_type=F32)


def _pair_masks():
    lane = lax.broadcasted_iota(jnp.int32, (1, PAIR_W), 1)
    row = lax.broadcasted_iota(jnp.int32, (PAIR_W, 1), 0)
    lo_lane = lane < RET_HEAD_DIM
    block_diag = (row < RET_HEAD_DIM) == lo_lane
    return lo_lane, block_diag


def _stack_heads(a, lo_lane):
    zero = jnp.zeros_like(a)
    return jnp.concatenate([jnp.where(lo_lane, a, zero), jnp.where(lo_lane, zero, a)], axis=0)


def _mod_kernel(cc_ref, w_ref, b_ref, o_ref):
    s = jax.nn.silu(cc_ref[...]).astype(BF16)
    o_ref[...] = _mm(s, w_ref[...].astype(BF16)) + b_ref[...]


def _run_mod(cc, w_ada, b_ada):
    nblk = (N_MOD * D_MODEL) // D_MODEL
    return pl.pallas_call(
        _mod_kernel,
        grid=(DEPTH, nblk),
        in_specs=[
            pl.BlockSpec((MOD_ROWS, D_MODEL), lambda l, j: (0, 0)),
            pl.BlockSpec((None, D_MODEL, D_MODEL), lambda l, j: (l, 0, j)),
            pl.BlockSpec((None, 1, D_MODEL), lambda l, j: (l, 0, j)),
        ],
        out_specs=pl.BlockSpec((None, MOD_ROWS, D_MODEL), lambda l, j: (l, 0, j)),
        out_shape=jax.ShapeDtypeStruct((DEPTH, MOD_ROWS, N_MOD * D_MODEL), F32),
        compiler_params=pltpu.CompilerParams(dimension_semantics=("arbitrary", "arbitrary")),
        name="ada_modulation",
    )(cc, w_ada, b_ada)


def _mix_in_kernel(x_ref, mod_ref, win_ref, convw_ref, glng_ref, glnb_ref, ws01_ref, ws23_ref, bs_ref,
                   decf_ref, decb_ref, sb0_ref, *rest, tile, conv_row, emit_states, states_only):
    if states_only:
        sf_out_ref, sb_out_ref, sb_scr = rest
    elif emit_states:
        ycg_ref, qkv_ref, g_ref, sb_ref, sf_out_ref, sb_out_ref, sb_scr = rest
    else:
        ycg_ref, qkv_ref, g_ref, sb_ref, sb_scr = rest
    n_chunk = tile // CHUNK
    t = pl.program_id(1)

    @pl.when(t == 0)
    def _():
        sb_scr[...] = sb0_ref[...]

    sh_m = mod_ref[:, 0 * D_MODEL:1 * D_MODEL]
    sc_m = mod_ref[:, 1 * D_MODEL:2 * D_MODEL]
    h = (x_ref[...] * (1.0 + sc_m) + sh_m).astype(BF16)

    if states_only:
        kv = _mm(h, win_ref[:, C_K:C_K + 2 * RET_DIM])
        k = kv[:, 0:RET_DIM]
        v_bf = kv[:, RET_DIM:2 * RET_DIM].astype(BF16)
    else:
        proj = _mm(h, win_ref[...])

        xin = proj[:, C_CONV:C_CONV + CONV_DIM]
        gate_b = proj[:, C_CONV + CONV_DIM:C_CONV + 2 * CONV_DIM]
        gate_c = proj[:, C_CONV + 2 * CONV_DIM:C_CONV + 3 * CONV_DIM]
        z = gate_c * xin
        pos_in_row = lax.broadcasted_iota(jnp.int32, (tile, 1), 0) & (conv_row - 1)
        z_prev = jnp.where(pos_in_row == 0, 0.0, pltpu.roll(z, 1, 0))
        z_next = jnp.where(pos_in_row == conv_row - 1, 0.0, pltpu.roll(z, tile - 1, 0))
        y_conv = gate_b * (convw_ref[0:1, :] * z_prev + convw_ref[1:2, :] * z + convw_ref[2:3, :] * z_next)
        ycg_ref[:, 0:CONV_DIM] = y_conv.astype(BF16)

        u_act = jax.nn.gelu(proj[:, C_GMLP:C_GMLP + GMLP_DIM])
        v_ln = _layer_norm(jax.nn.gelu(proj[:, C_GMLP + GMLP_DIM:C_GMLP + 2 * GMLP_DIM]),
                           glng_ref[...], glnb_ref[...]).astype(BF16)
        lane = lax.broadcasted_iota(jnp.int32, (1, GMLP_DIM), 1)
        head_w = GMLP_DIM // GMLP_HEADS
        in_head = [(lane >= hh * head_w) & (lane < (hh + 1) * head_w) for hh in range(GMLP_HEADS)]
        zero_bf = jnp.zeros((CHUNK, GMLP_DIM), BF16)
        for c in range(n_chunk):
            rows = slice(c * CHUNK, (c + 1) * CHUNK)
            vc = v_ln[rows, :]
            by_head = [jnp.where(in_head[hh], vc, zero_bf) for hh in range(GMLP_HEADS)]
            mixed = (_mm(ws01_ref[...], jnp.concatenate(by_head[0:2], axis=0))
                     + _mm(ws23_ref[...], jnp.concatenate(by_head[2:4], axis=0)) + bs_ref[...])
            ycg_ref[rows, CONV_DIM:CONV_DIM + GMLP_DIM] = (u_act[rows, :] * mixed).astype(BF16)

        qkv_ref[:, 0:RET_DIM] = (proj[:, C_Q:C_Q + RET_DIM] * RET_SCALE).astype(BF16)
        k = proj[:, C_K:C_K + RET_DIM]
        qkv_ref[:, RET_DIM:2 * RET_DIM] = k.astype(BF16)
        v_bf = proj[:, C_V:C_V + RET_DIM].astype(BF16)
        qkv_ref[:, 2 * RET_DIM:3 * RET_DIM] = v_bf
        g_ref[...] = proj[:, C_G:C_G + RET_DIM]

    _, block_diag = _pair_masks()
    pos = lax.broadcasted_iota(jnp.int32, (CHUNK, RET_DIM), 0).astype(F32)
    lg_b = -jnp.exp(decb_ref[...])
    w_b = jnp.exp(lg_b * pos)
    decay_chunk_b = jnp.exp(lg_b * float(CHUNK))
    if emit_states:
        lg_f = -jnp.exp(decf_ref[...])
        w_f = jnp.exp(lg_f * (float(CHUNK - 1) - pos))
        decay_chunk_f = jnp.exp(lg_f * float(CHUNK))
    for c in reversed(range(n_chunk)):
        rows = slice(c * CHUNK, (c + 1) * CHUNK)
        for p in range(N_PAIR):
            cols = slice(p * PAIR_W, (p + 1) * PAIR_W)
            s_old = sb_scr[cols, :]
            if not states_only:
                sb_ref[c, cols, :] = s_old.astype(BF16)
            kw = (k[rows, cols] * w_b[:, cols]).astype(BF16)
            upd = _mm_tn(kw, v_bf[rows, cols])
            sb_scr[cols, :] = s_old * decay_chunk_b[:, cols] + jnp.where(block_diag, upd, 0.0)

    if emit_states:
        sb_out_ref[...] = sb_scr[...]
        for p in range(N_PAIR):
            cols = slice(p * PAIR_W, (p + 1) * PAIR_W)
            s_f = jnp.zeros((PAIR_W, PAIR_W), F32)
            for c in range(n_chunk):
                rows = slice(c * CHUNK, (c + 1) * CHUNK)
                kw = (k[rows, cols] * w_f[:, cols]).astype(BF16)
                upd = _mm_tn(kw, v_bf[rows, cols])
                s_f = s_f * decay_chunk_f[:, cols] + jnp.where(block_diag, upd, 0.0)
            sf_out_ref[cols, :] = s_f


def _run_mix_in(x, layer, mod, win, convw, glng, glnb, ws01, ws23, bs_tile, decf, decb, sb0, *,
                tile, conv_row, mod_row, emit_states, states_only=False):
    bsz, n, _ = x.shape
    nt = n // tile
    n_chunk = tile // CHUNK
    if emit_states:
        assert nt == 1
    assert emit_states or not states_only
    rev = lambda b, t: (b, nt - 1 - t, 0)
    per_layer = lambda b, t: (layer, 0, 0)
    mod_map = (lambda b, t: (layer, b, 0, 0)) if mod_row is None else (lambda b, t: (layer, mod_row, 0, 0))
    state_map = lambda b, t: (b, 0, 0)
    out_shape, out_specs = [], []
    if not states_only:
        out_shape += [
            jax.ShapeDtypeStruct((bsz, n, CONV_DIM + GMLP_DIM), BF16),
            jax.ShapeDtypeStruct((bsz, n, 3 * RET_DIM), BF16),
            jax.ShapeDtypeStruct((bsz, n, RET_DIM), F32),
            jax.ShapeDtypeStruct((bsz, n // CHUNK, RET_DIM, PAIR_W), BF16),
        ]
        out_specs += [
            pl.BlockSpec((None, tile, CONV_DIM + GMLP_DIM), rev),
            pl.BlockSpec((None, tile, 3 * RET_DIM), rev),
            pl.BlockSpec((None, tile, RET_DIM), rev),
            pl.BlockSpec((None, n_chunk, RET_DIM, PAIR_W), lambda b, t: (b, nt - 1 - t, 0, 0)),
        ]
    if emit_states:
        out_shape += [jax.ShapeDtypeStruct((bsz, RET_DIM, PAIR_W), F32)] * 2
        out_specs += [pl.BlockSpec((None, RET_DIM, PAIR_W), state_map)] * 2
    return pl.pallas_call(
        functools.partial(_mix_in_kernel, tile=tile, conv_row=conv_row, emit_states=emit_states,
                          states_only=states_only),
        grid=(bsz, nt),
        in_specs=[
            pl.BlockSpec((None, tile, D_MODEL), rev),
            pl.BlockSpec((None, None, 1, N_MOD * D_MODEL), mod_map),
            pl.BlockSpec((None, D_MODEL, D_IN), per_layer),
            pl.BlockSpec((None, 3, CONV_DIM), per_layer),
            pl.BlockSpec((None, 1, GMLP_DIM), per_layer),
            pl.BlockSpec((None, 1, GMLP_DIM), per_layer),
            pl.BlockSpec((None, CHUNK, 2 * CHUNK), per_layer),
            pl.BlockSpec((None, CHUNK, 2 * CHUNK), per_layer),
            pl.BlockSpec((None, CHUNK, GMLP_DIM), per_layer),
            pl.BlockSpec((None, 1, RET_DIM), per_layer),
            pl.BlockSpec((None, 1, RET_DIM), per_layer),
            pl.BlockSpec((None, RET_DIM, PAIR_W), state_map),
        ],
        out_specs=out_specs,
        out_shape=out_shape,
        scratch_shapes=[pltpu.VMEM((RET_DIM, PAIR_W), F32)],
        compiler_params=pltpu.CompilerParams(
            dimension_semantics=("arbitrary", "arbitrary"), vmem_limit_bytes=VMEM_LIMIT_BYTES),
        name="ctx_states" if states_only else ("mix_in_ctx" if emit_states else "mix_in"),
    )(x, mod, win, convw, glng, glnb, ws01, ws23, bs_tile, decf, decb, sb0)


def _mix_out_kernel(x_ref, ycg_ref, qkv_ref, g_ref, sb_ref, sf0_ref, mod_ref, lng_ref, lnb_ref,
                    wout_ref, wff1_ref, wff2_ref, decf_ref, decb_ref, decf2_ref, decb2_ref,
                    o_ref, sf_scr, d2_scr, wq_scr, y_scr, hid_scr, *, tile, sub, tiles_per_seq, n_tiles):
    n_chunk = tile // CHUNK
    s = pl.program_id(0)
    t_ret = lax.rem(jnp.minimum(s, n_tiles - 1), tiles_per_seq)

    lg_f = -jnp.exp(decf_ref[...])
    lg_b = -jnp.exp(decb_ref[...])

    @pl.when(s == 0)
    def _():
        pos = lax.broadcasted_iota(jnp.int32, (CHUNK, RET_DIM), 0).astype(F32)
        wq_scr[0] = jnp.exp(lg_f * (pos + 1.0))
        wq_scr[1] = jnp.exp(lg_b * (float(CHUNK) - pos))
        wq_scr[2] = jnp.exp(lg_f * (float(CHUNK - 1) - pos))
        i = lax.broadcasted_iota(jnp.int32, (CHUNK, 2 * CHUNK), 0)
        j = lax.broadcasted_iota(jnp.int32, (CHUNK, 2 * CHUNK), 1) & (CHUNK - 1)
        rel = i - j
        causal = rel >= 0
        relf = rel.astype(F32)
        for p in range(N_PAIR):
            lf = -jnp.exp(decf2_ref[p])
            lb = -jnp.exp(decb2_ref[p])
            d2_scr[p] = jnp.where(causal, jnp.exp(lf * jnp.where(causal, relf, 0.0)),
                                  jnp.exp(lb * jnp.where(causal, 0.0, -relf)))

    @pl.when(t_ret == 0)
    def _():
        sf_scr[...] = sf0_ref[...]

    lo_lane, block_diag = _pair_masks()
    decay_chunk_f = jnp.exp(lg_f * float(CHUNK))
    inv_d = 1.0 / RET_HEAD_DIM

    gt_m = mod_ref[:, 2 * D_MODEL:3 * D_MODEL]
    sh_f = mod_ref[:, 3 * D_MODEL:4 * D_MODEL]
    sc_f = mod_ref[:, 4 * D_MODEL:5 * D_MODEL]
    gt_f = mod_ref[:, 5 * D_MODEL:6 * D_MODEL]

    units = [(c, p) for c in range(n_chunk) for p in range(N_PAIR)]
    rows = lambda c: slice(c * CHUNK, (c + 1) * CHUNK)
    cols = lambda p: slice(p * PAIR_W, (p + 1) * PAIR_W)
    q_of = lambda c, p: qkv_ref[rows(c), p * PAIR_W:(p + 1) * PAIR_W]
    k_of = lambda c, p: qkv_ref[rows(c), RET_DIM + p * PAIR_W:RET_DIM + (p + 1) * PAIR_W]
    v_of = lambda c, p: qkv_ref[rows(c), 2 * RET_DIM + p * PAIR_W:2 * RET_DIM + (p + 1) * PAIR_W]
    blocks = [slice(i * sub, (i + 1) * sub) for i in range(tile // sub)]

    def step(do_ret, do_mm):
        if do_ret:
            scores = {u: _mm_nt(q_of(*u), _stack_heads(k_of(*u), lo_lane)) for u in units}
            upd = {}
            for c, p in units:
                kw = (k_of(c, p).astype(F32) * wq_scr[2, :, cols(p)]).astype(BF16)
                upd[c, p] = _mm_tn(kw, v_of(c, p))
            cross_b = {(c, p): _mm(q_of(c, p), sb_ref[c, cols(p), :]) for c, p in units}

        if do_mm:
            z = [_mm(y_scr[r, :], wout_ref[...]) for r in blocks]

        if do_ret:
            s_in = {}
            for p in range(N_PAIR):
                s_f = sf_scr[cols(p), :]
                for c in range(n_chunk):
                    s_in[c, p] = s_f
                    s_f = s_f * decay_chunk_f[:, cols(p)] + jnp.where(block_diag, upd[c, p], 0.0)
                sf_scr[cols(p), :] = s_f
            o_in = {(c, p): _mm((scores[c, p] * d2_scr[p]).astype(BF16), _stack_heads(v_of(c, p), lo_lane))
                    for c, p in units}
            cross_f = {u: _mm(q_of(*u), s_in[u].astype(BF16)) for u in units}

        if do_mm:
            for r, z_r in zip(blocks, z):
                x1 = _layer_norm(ALPHA * x_ref[r, :] + gt_m * z_r, lng_ref[0:1, :], lnb_ref[0:1, :])
                o_ref[r, :] = x1
                hf = (x1 * (1.0 + sc_f) + sh_f).astype(BF16)
                for nb in range(D_FF // FF_BLOCK):
                    ff_cols = slice(nb * FF_BLOCK, (nb + 1) * FF_BLOCK)
                    a = jnp.maximum(_mm(hf, wff1_ref[:, ff_cols]), 0.0)
                    hid_scr[r, ff_cols] = (a * a).astype(BF16)

        if do_ret:
            y_scr[:, 0:CONV_DIM + GMLP_DIM] = ycg_ref[...]
            for c, p in units:
                o = o_in[c, p] + wq_scr[0, :, cols(p)] * cross_f[c, p] + wq_scr[1, :, cols(p)] * cross_b[c, p]
                mu = jnp.where(lo_lane,
                               jnp.sum(jnp.where(lo_lane, o, 0.0), axis=-1, keepdims=True),
                               jnp.sum(jnp.where(lo_lane, 0.0, o), axis=-1, keepdims=True)) * inv_d
                d = o - mu
                dd = d * d
                var = jnp.where(lo_lane,
                                jnp.sum(jnp.where(lo_lane, dd, 0.0), axis=-1, keepdims=True),
                                jnp.sum(jnp.where(lo_lane, 0.0, dd), axis=-1, keepdims=True)) * inv_d
                o_n = d * lax.rsqrt(var + LN_EPS)
                y_scr[rows(c), RET_DIM + p * PAIR_W:RET_DIM + (p + 1) * PAIR_W] = (
                    jax.nn.silu(g_ref[rows(c), cols(p)]) * o_n).astype(BF16)

        if do_mm:
            y_ff = [_mm(hid_scr[r, :], wff2_ref[...]) for r in blocks]
            for r, y_r in zip(blocks, y_ff):
                o_ref[r, :] = _layer_norm(ALPHA * o_ref[r, :] + gt_f * y_r, lng_ref[1:2, :], lnb_ref[1:2, :])

    pl.when(s == 0)(lambda: step(True, False))
    pl.when((s > 0) & (s < n_tiles))(lambda: step(True, True))
    pl.when(s == n_tiles)(lambda: step(False, True))


def _run_mix_out(x, ycg, qkv, g, sb, sf0, layer, mod, lng, lnb, wout, wff1, wff2, decf, decb, decf2, decb2, *,
                 tile, mod_row):
    bsz, n, _ = x.shape
    nt = n // tile
    n_tiles = bsz * nt
    n_chunk = tile // CHUNK

    def ret_tile(s):
        i = jnp.minimum(s, n_tiles - 1)
        return lax.div(i, nt), lax.rem(i, nt)

    def mm_tile(s):
        i = jnp.maximum(s - 1, 0)
        return lax.div(i, nt), lax.rem(i, nt)

    ret3 = lambda s: (*ret_tile(s), 0)
    mm3 = lambda s: (*mm_tile(s), 0)
    per_layer = lambda s: (layer, 0, 0)
    per_layer4 = lambda s: (layer, 0, 0, 0)
    mod_map = ((lambda s: (layer, mm_tile(s)[0], 0, 0)) if mod_row is None
               else (lambda s: (layer, mod_row, 0, 0)))
    resident = dict(pipeline_mode=pl.Buffered(1))
    return pl.pallas_call(
        functools.partial(_mix_out_kernel, tile=tile, sub=min(tile, SUB_ROWS), tiles_per_seq=nt, n_tiles=n_tiles),
        grid=(n_tiles + 1,),
        in_specs=[
            pl.BlockSpec((None, tile, D_MODEL), mm3),
            pl.BlockSpec((None, tile, CONV_DIM + GMLP_DIM), ret3),
            pl.BlockSpec((None, tile, 3 * RET_DIM), ret3),
            pl.BlockSpec((None, tile, RET_DIM), ret3),
            pl.BlockSpec((None, n_chunk, RET_DIM, PAIR_W), lambda s: (*ret_tile(s), 0, 0)),
            pl.BlockSpec((None, RET_DIM, PAIR_W), lambda s: (ret_tile(s)[0], 0, 0)),
            pl.BlockSpec((None, None, 1, N_MOD * D_MODEL), mod_map),
            pl.BlockSpec((None, 2, D_MODEL), per_layer),
            pl.BlockSpec((None, 2, D_MODEL), per_layer),
            pl.BlockSpec((None, D_MODEL, D_MODEL), per_layer, **resident),
            pl.BlockSpec((None, D_MODEL, D_FF), per_layer, **resident),
            pl.BlockSpec((None, D_FF, D_MODEL), per_layer, **resident),
            pl.BlockSpec((None, 1, RET_DIM), per_layer),
            pl.BlockSpec((None, 1, RET_DIM), per_layer),
            pl.BlockSpec((None, N_PAIR, 1, 2 * CHUNK), per_layer4),
            pl.BlockSpec((None, N_PAIR, 1, 2 * CHUNK), per_layer4),
        ],
        out_specs=pl.BlockSpec((None, tile, D_MODEL), mm3),
        out_shape=jax.ShapeDtypeStruct((bsz, n, D_MODEL), F32),
        scratch_shapes=[
            pltpu.VMEM((RET_DIM, PAIR_W), F32),
            pltpu.VMEM((N_PAIR, CHUNK, 2 * CHUNK), F32),
            pltpu.VMEM((3, CHUNK, RET_DIM), F32),
            pltpu.VMEM((tile, D_MODEL), BF16),
            pltpu.VMEM((tile, D_FF), BF16),
        ],
        compiler_params=pltpu.CompilerParams(
            dimension_semantics=("arbitrary",), vmem_limit_bytes=VMEM_LIMIT_BYTES),
        name="mix_out",
    )(x, ycg, qkv, g, sb, sf0, mod, lng, lnb, wout, wff1, wff2, decf, decb, decf2, decb2)


def kernel(x, c, ctx, c_ctx, w_ada, b_ada, w_in, conv_w, gmlp_ln_g, gmlp_ln_b, gmlp_ws, gmlp_bs,
           ret_decay_fwd, ret_decay_bwd, w_out, w_ff1, w_ff2, ln_g, ln_b):
    bsz, n_lat, _ = x.shape
    ctx_len = ctx.shape[1]
    lat_tile = 1024
    out_tile = 512
    assert n_lat % lat_tile == 0 and lat_tile % CHUNK == 0 and ctx_len % CHUNK == 0
    assert n_lat % out_tile == 0 and out_tile % SUB_ROWS == 0
    assert ctx_len & (ctx_len - 1) == 0 and GRID_W & (GRID_W - 1) == 0

    win_b = w_in.astype(BF16)
    wout_b = w_out.astype(BF16)
    wff1_b = w_ff1.astype(BF16)
    wff2_b = w_ff2.astype(BF16)

    cc = jnp.concatenate([c, c_ctx[None, :], jnp.zeros((MOD_ROWS - bsz - 1, D_MODEL), F32)], axis=0)
    mod = _run_mod(cc, w_ada, b_ada.reshape(DEPTH, 1, N_MOD * D_MODEL))
    mod = mod.reshape(DEPTH, MOD_ROWS, 1, N_MOD * D_MODEL)

    ws = gmlp_ws.astype(BF16)
    ws01 = jnp.concatenate([ws[:, 0], ws[:, 1]], axis=2)
    ws23 = jnp.concatenate([ws[:, 2], ws[:, 3]], axis=2)
    bs_tile = jnp.repeat(jnp.swapaxes(gmlp_bs, 1, 2), GMLP_DIM // GMLP_HEADS, axis=2)
    decf = jnp.repeat(ret_decay_fwd.astype(F32), RET_HEAD_DIM, axis=1)[:, None, :]
    decb = jnp.repeat(ret_decay_bwd.astype(F32), RET_HEAD_DIM, axis=1)[:, None, :]
    decf2 = jnp.repeat(ret_decay_fwd.astype(F32), CHUNK, axis=1).reshape(DEPTH, N_PAIR, 1, 2 * CHUNK)
    decb2 = jnp.repeat(ret_decay_bwd.astype(F32), CHUNK, axis=1).reshape(DEPTH, N_PAIR, 1, 2 * CHUNK)
    mix_consts = (mod, win_b, conv_w, gmlp_ln_g[:, None, :], gmlp_ln_b[:, None, :], ws01, ws23, bs_tile, decf, decb)
    out_consts = (mod, ln_g, ln_b, wout_b, wff1_b, wff2_b, decf, decb, decf2, decb2)

    zero_state = jnp.zeros((bsz, RET_DIM, PAIR_W), F32)
    xc = ctx
    for l in range(DEPTH):
        last = l == DEPTH - 1
        ctx_out = _run_mix_in(xc, l, *mix_consts, zero_state, tile=ctx_len, conv_row=ctx_len,
                              mod_row=CTX_MOD_ROW, emit_states=True, states_only=last)
        sf_ctx, sb_ctx = ctx_out[-2:]
        ycg, qkv, g, sb = _run_mix_in(x, l, *mix_consts, sb_ctx, tile=lat_tile, conv_row=GRID_W,
                                      mod_row=None, emit_states=False)
        x = _run_mix_out(x, ycg, qkv, g, sb, sf_ctx, l, *out_consts, tile=out_tile, mod_row=None)
        if not last:
            ycg_c, qkv_c, g_c, sb_c = ctx_out[:4]
            xc = _run_mix_out(xc, ycg_c, qkv_c, g_c, sb_c, zero_state, l, *out_consts,
                              tile=ctx_len, mod_row=CTX_MOD_ROW)
    return x
```

```python
import functools

import jax
import jax.numpy as jnp
from jax import lax
from jax.experimental import pallas as pl
from jax.experimental.pallas import tpu as pltpu

D_MODEL = 1024
DEPTH = 2
GRID_W = 64
CONV_DIM = 256
GMLP_DIM = 256
GMLP_HEADS = 4
CHUNK = 128
RET_DIM = 512
RET_HEADS = 8
RET_HEAD_DIM = 64
RET_SCALE = RET_HEAD_DIM ** -0.5
D_FF = 4 * D_MODEL
N_MOD = 6
LN_EPS = 1e-5
ALPHA = (2 * DEPTH) ** 0.25
D_IN = 3 * CONV_DIM + 2 * GMLP_DIM + 4 * RET_DIM

C_CONV = 0
C_GMLP = 3 * CONV_DIM
C_Q = C_GMLP + 2 * GMLP_DIM
C_K = C_Q + RET_DIM
C_V = C_K + RET_DIM
C_G = C_V + RET_DIM

N_PAIR = RET_HEADS // 2
PAIR_W = 2 * RET_HEAD_DIM
MOD_ROWS = 16
CTX_MOD_ROW = 8
FF_BLOCK = 1024

VMEM_LIMIT_BYTES = 56 * 1024 * 1024

F32 = jnp.float32
BF16 = jnp.bfloat16


def _layer_norm(v, g, b):
    mu = jnp.mean(v, axis=-1, keepdims=True)
    d = v - mu
    var = jnp.mean(d * d, axis=-1, keepdims=True)
    return d * lax.rsqrt(var + LN_EPS) * g + b


def _mm(a, b):
    return jnp.dot(a, b, preferred_element_type=F32)


def _mm_nt(a, b):
    return lax.dot_general(a, b, (((1,), (1,)), ((), ())), preferred_element_type=F32)


def _mm_tn(a, b):
    return lax.dot_general(a, b, (((0,), (0,)), ((), ())), preferred_element_type=F32)


def _pair_masks():
    lane = lax.broadcasted_iota(jnp.int32, (1, PAIR_W), 1)
    row = lax.broadcasted_iota(jnp.int32, (PAIR_W, 1), 0)
    lo_lane = lane < RET_HEAD_DIM
    block_diag = (row < RET_HEAD_DIM) == lo_lane
    return lo_lane, block_diag


def _stack_heads(a, lo_lane):
    zero = jnp.zeros_like(a)
    return jnp.concatenate([jnp.where(lo_lane, a, zero), jnp.where(lo_lane, zero, a)], axis=0)


def _mod_kernel(cc_ref, w_ref, b_ref, o_ref):
    s = jax.nn.silu(cc_ref[...]).astype(BF16)
    o_ref[...] = _mm(s, w_ref[...].astype(BF16)) + b_ref[...]


def _run_mod(cc, w_ada, b_ada):
    nblk = (N_MOD * D_MODEL) // D_MODEL
    return pl.pallas_call(
        _mod_kernel,
        grid=(DEPTH, nblk),
        in_specs=[
            pl.BlockSpec((MOD_ROWS, D_MODEL), lambda l, j: (0, 0)),
            pl.BlockSpec((None, D_MODEL, D_MODEL), lambda l, j: (l, 0, j)),
            pl.BlockSpec((None, 1, D_MODEL), lambda l, j: (l, 0, j)),
        ],
        out_specs=pl.BlockSpec((None, MOD_ROWS, D_MODEL), lambda l, j: (l, 0, j)),
        out_shape=jax.ShapeDtypeStruct((DEPTH, MOD_ROWS, N_MOD * D_MODEL), F32),
        compiler_params=pltpu.CompilerParams(dimension_semantics=("arbitrary", "arbitrary")),
        name="ada_modulation",
    )(cc, w_ada, b_ada)


def _mix_in_kernel(x_ref, mod_ref, win_ref, convw_ref, glng_ref, glnb_ref, ws01_ref, ws23_ref, bs_ref,
                   decf_ref, decb_ref, sb0_ref, *rest, tile, conv_row, emit_states, states_only):
    if states_only:
        sf_out_ref, sb_out_ref, sb_scr = rest
    elif emit_states:
        ycg_ref, qkv_ref, g_ref, sb_ref, sf_out_ref, sb_out_ref, sb_scr = rest
    else:
        ycg_ref, qkv_ref, g_ref, sb_ref, sb_scr = rest
    n_chunk = tile // CHUNK
    t = pl.program_id(1)

    @pl.when(t == 0)
    def _():
        sb_scr[...] = sb0_ref[...]

    sh_m = mod_ref[:, 0 * D_MODEL:1 * D_MODEL]
    sc_m = mod_ref[:, 1 * D_MODEL:2 * D_MODEL]
    h = (x_ref[...] * (1.0 + sc_m) + sh_m).astype(BF16)

    if states_only:
        kv = _mm(h, win_ref[:, C_K:C_K + 2 * RET_DIM])
        k = kv[:, 0:RET_DIM]
        v_bf = kv[:, RET_DIM:2 * RET_DIM].astype(BF16)
    else:
        proj = _mm(h, win_ref[...])

        xin = proj[:, C_CONV:C_CONV + CONV_DIM]
        gate_b = proj[:, C_CONV + CONV_DIM:C_CONV + 2 * CONV_DIM]
        gate_c = proj[:, C_CONV + 2 * CONV_DIM:C_CONV + 3 * CONV_DIM]
        z = gate_c * xin
        pos_in_row = lax.broadcasted_iota(jnp.int32, (tile, 1), 0) & (conv_row - 1)
        z_prev = jnp.where(pos_in_row == 0, 0.0, pltpu.roll(z, 1, 0))
        z_next = jnp.where(pos_in_row == conv_row - 1, 0.0, pltpu.roll(z, tile - 1, 0))
        y_conv = gate_b * (convw_ref[0:1, :] * z_prev + convw_ref[1:2, :] * z + convw_ref[2:3, :] * z_next)
        ycg_ref[:, 0:CONV_DIM] = y_conv.astype(BF16)

        u_act = jax.nn.gelu(proj[:, C_GMLP:C_GMLP + GMLP_DIM])
        v_ln = _layer_norm(jax.nn.gelu(proj[:, C_GMLP + GMLP_DIM:C_GMLP + 2 * GMLP_DIM]),
                           glng_ref[...], glnb_ref[...]).astype(BF16)
        lane = lax.broadcasted_iota(jnp.int32, (1, GMLP_DIM), 1)
        head_w = GMLP_DIM // GMLP_HEADS
        in_head = [(lane >= hh * head_w) & (lane < (hh + 1) * head_w) for hh in range(GMLP_HEADS)]
        zero_bf = jnp.zeros((CHUNK, GMLP_DIM), BF16)
        for c in range(n_chunk):
            rows = slice(c * CHUNK, (c + 1) * CHUNK)
            vc = v_ln[rows, :]
            by_head = [jnp.where(in_head[hh], vc, zero_bf) for hh in range(GMLP_HEADS)]
            mixed = (_mm(ws01_ref[...], jnp.concatenate(by_head[0:2], axis=0))
                     + _mm(ws23_ref[...], jnp.concatenate(by_head[2:4], axis=0)) + bs_ref[...])
            ycg_ref[rows, CONV_DIM:CONV_DIM + GMLP_DIM] = (u_act[rows, :] * mixed).astype(BF16)

        qkv_ref[:, 0:RET_DIM] = (proj[:, C_Q:C_Q + RET_DIM] * RET_SCALE).astype(BF16)
        k = proj[:, C_K:C_K + RET_DIM]
        qkv_ref[:, RET_DIM:2 * RET_DIM] = k.astype(BF16)
        v_bf = proj[:, C_V:C_V + RET_DIM].astype(BF16)
        qkv_ref[:, 2 * RET_DIM:3 * RET_DIM] = v_bf
        g_ref[...] = proj[:, C_G:C_G + RET_DIM]

    _, block_diag = _pair_masks()
    pos = lax.broadcasted_iota(jnp.int32, (CHUNK, RET_DIM), 0).astype(F32)
    lg_b = -jnp.exp(decb_ref[...])
    w_b = jnp.exp(lg_b * pos)
    decay_chunk_b = jnp.exp(lg_b * float(CHUNK))
    if emit_states:
        lg_f = -jnp.exp(decf_ref[...])
        w_f = jnp.exp(lg_f * (float(CHUNK - 1) - pos))
        decay_chunk_f = jnp.exp(lg_f * float(CHUNK))
    for c in reversed(range(n_chunk)):
        rows = slice(c * CHUNK, (c + 1) * CHUNK)
        for p in range(N_PAIR):
            cols = slice(p * PAIR_W, (p + 1) * PAIR_W)
            s_old = sb_scr[cols, :]
            if not states_only:
                sb_ref[c, cols, :] = s_old.astype(BF16)
            kw = (k[rows, cols] * w_b[:, cols]).astype(BF16)
            upd = _mm_tn(kw, v_bf[rows, cols])
            sb_scr[cols, :] = s_old * decay_chunk_b[:, cols] + jnp.where(block_diag, upd, 0.0)

    if emit_states:
        sb_out_ref[...] = sb_scr[...]
        for p in range(N_PAIR):
            cols = slice(p * PAIR_W, (p + 1) * PAIR_W)
            s_f = jnp.zeros((PAIR_W, PAIR_W), F32)
            for c in range(n_chunk):
                rows = slice(c * CHUNK, (c + 1) * CHUNK)
                kw = (k[rows, cols] * w_f[:, cols]).astype(BF16)
                upd = _mm_tn(kw, v_bf[rows, cols])
                s_f = s_f * decay_chunk_f[:, cols] + jnp.where(block_diag, upd, 0.0)
            sf_out_ref[cols, :] = s_f


def _run_mix_in(x, layer, mod, win, convw, glng, glnb, ws01, ws23, bs_tile, decf, decb, sb0, *,
                tile, conv_row, mod_row, emit_states, states_only=False):
    bsz, n, _ = x.shape
    nt = n // tile
    n_chunk = tile // CHUNK
    if emit_states:
        assert nt == 1
    assert emit_states or not states_only
    rev = lambda b, t: (b, nt - 1 - t, 0)
    per_layer = lambda b, t: (layer, 0, 0)
    mod_map = (lambda b, t: (layer, b, 0, 0)) if mod_row is None else (lambda b, t: (layer, mod_row, 0, 0))
    state_map = lambda b, t: (b, 0, 0)
    out_shape, out_specs = [], []
    if not states_only:
        out_shape += [
            jax.ShapeDtypeStruct((bsz, n, CONV_DIM + GMLP_DIM), BF16),
            jax.ShapeDtypeStruct((bsz, n, 3 * RET_DIM), BF16),
            jax.ShapeDtypeStruct((bsz, n, RET_DIM), F32),
            jax.ShapeDtypeStruct((bsz, n // CHUNK, RET_DIM, PAIR_W), BF16),
        ]
        out_specs += [
            pl.BlockSpec((None, tile, CONV_DIM + GMLP_DIM), rev),
            pl.BlockSpec((None, tile, 3 * RET_DIM), rev),
            pl.BlockSpec((None, tile, RET_DIM), rev),
            pl.BlockSpec((None, n_chunk, RET_DIM, PAIR_W), lambda b, t: (b, nt - 1 - t, 0, 0)),
        ]
    if emit_states:
        out_shape += [jax.ShapeDtypeStruct((bsz, RET_DIM, PAIR_W), F32)] * 2
        out_specs += [pl.BlockSpec((None, RET_DIM, PAIR_W), state_map)] * 2
    return pl.pallas_call(
        functools.partial(_mix_in_kernel, tile=tile, conv_row=conv_row, emit_states=emit_states,
                          states_only=states_only),
        grid=(bsz, nt),
        in_specs=[
            pl.BlockSpec((None, tile, D_MODEL), rev),
            pl.BlockSpec((None, None, 1, N_MOD * D_MODEL), mod_map),
            pl.BlockSpec((None, D_MODEL, D_IN), per_layer),
            pl.BlockSpec((None, 3, CONV_DIM), per_layer),
            pl.BlockSpec((None, 1, GMLP_DIM), per_layer),
            pl.BlockSpec((None, 1, GMLP_DIM), per_layer),
            pl.BlockSpec((None, CHUNK, 2 * CHUNK), per_layer),
            pl.BlockSpec((None, CHUNK, 2 * CHUNK), per_layer),
            pl.BlockSpec((None, CHUNK, GMLP_DIM), per_layer),
            pl.BlockSpec((None, 1, RET_DIM), per_layer),
            pl.BlockSpec((None, 1, RET_DIM), per_layer),
            pl.BlockSpec((None, RET_DIM, PAIR_W), state_map),
        ],
        out_specs=out_specs,
        out_shape=out_shape,
        scratch_shapes=[pltpu.VMEM((RET_DIM, PAIR_W), F32)],
        compiler_params=pltpu.CompilerParams(
            dimension_semantics=("arbitrary", "arbitrary"), vmem_limit_bytes=VMEM_LIMIT_BYTES),
        name="ctx_states" if states_only else ("mix_in_ctx" if emit_states else "mix_in"),
    )(x, mod, win, convw, glng, glnb, ws01, ws23, bs_tile, decf, decb, sb0)


def _mix_out_kernel(xa_ref, xb_ref, ycg_ref, qkv_ref, g_ref, sb_ref, sf0_ref, mod_ret_ref, mod_mm_ref,
                    lng_ref, lnb_ref, wout_ref, wff1_ref, wff2_ref, decf_ref, decb_ref, decf2_ref, decb2_ref,
                    o_ref, sf_scr, d2_scr, wq_scr, y_scr, hid_scr, zb_scr, x1a_scr, hfa_scr,
                    *, tile, tiles_per_seq, n_tiles):
    n_chunk = tile // CHUNK
    half = tile // 2
    blk_a, blk_b = slice(0, half), slice(half, tile)
    s = pl.program_id(0)
    t_ret = lax.rem(jnp.minimum(s, n_tiles - 1), tiles_per_seq)

    lg_f = -jnp.exp(decf_ref[...])
    lg_b = -jnp.exp(decb_ref[...])

    @pl.when(s == 0)
    def _():
        pos = lax.broadcasted_iota(jnp.int32, (CHUNK, RET_DIM), 0).astype(F32)
        wq_scr[0] = jnp.exp(lg_f * (pos + 1.0))
        wq_scr[1] = jnp.exp(lg_b * (float(CHUNK) - pos))
        wq_scr[2] = jnp.exp(lg_f * (float(CHUNK - 1) - pos))
        i = lax.broadcasted_iota(jnp.int32, (CHUNK, 2 * CHUNK), 0)
        j = lax.broadcasted_iota(jnp.int32, (CHUNK, 2 * CHUNK), 1) & (CHUNK - 1)
        rel = i - j
        causal = rel >= 0
        relf = rel.astype(F32)
        for p in range(N_PAIR):
            lf = -jnp.exp(decf2_ref[p])
            lb = -jnp.exp(decb2_ref[p])
            d2_scr[p] = jnp.where(causal, jnp.exp(lf * jnp.where(causal, relf, 0.0)),
                                  jnp.exp(lb * jnp.where(causal, 0.0, -relf)))

    @pl.when(t_ret == 0)
    def _():
        sf_scr[...] = sf0_ref[...]

    lo_lane, block_diag = _pair_masks()
    decay_chunk_f = jnp.exp(lg_f * float(CHUNK))
    inv_d = 1.0 / RET_HEAD_DIM

    def mods(ref):
        return (ref[:, 2 * D_MODEL:3 * D_MODEL], ref[:, 3 * D_MODEL:4 * D_MODEL],
                ref[:, 4 * D_MODEL:5 * D_MODEL], ref[:, 5 * D_MODEL:6 * D_MODEL])

    units = [(c, p) for c in range(n_chunk) for p in range(N_PAIR)]
    rows = lambda c: slice(c * CHUNK, (c + 1) * CHUNK)
    cols = lambda p: slice(p * PAIR_W, (p + 1) * PAIR_W)
    q_of = lambda c, p: qkv_ref[rows(c), p * PAIR_W:(p + 1) * PAIR_W]
    k_of = lambda c, p: qkv_ref[rows(c), RET_DIM + p * PAIR_W:RET_DIM + (p + 1) * PAIR_W]
    v_of = lambda c, p: qkv_ref[rows(c), 2 * RET_DIM + p * PAIR_W:2 * RET_DIM + (p + 1) * PAIR_W]

    def ff1(hf, r):
        for nb in range(D_FF // FF_BLOCK):
            ff_cols = slice(nb * FF_BLOCK, (nb + 1) * FF_BLOCK)
            a = jnp.maximum(_mm(hf, wff1_ref[:, ff_cols]), 0.0)
            hid_scr[r, ff_cols] = (a * a).astype(BF16)

    def step(do_ret, do_mm):
        if do_mm:
            gt_m, sh_f, sc_f, gt_f = mods(mod_mm_ref)
            ff1(hfa_scr[...], blk_a)

        if do_ret:
            scores = {u: _mm_nt(q_of(*u), _stack_heads(k_of(*u), lo_lane)) for u in units}
            upd = {}
            for c, p in units:
                kw = (k_of(c, p).astype(F32) * wq_scr[2, :, cols(p)]).astype(BF16)
                upd[c, p] = _mm_tn(kw, v_of(c, p))
            cross_b = {(c, p): _mm(q_of(c, p), sb_ref[c, cols(p), :]) for c, p in units}
            s_in = {}
            for p in range(N_PAIR):
                s_f = sf_scr[cols(p), :]
                for c in range(n_chunk):
                    s_in[c, p] = s_f
                    s_f = s_f * decay_chunk_f[:, cols(p)] + jnp.where(block_diag, upd[c, p], 0.0)
                sf_scr[cols(p), :] = s_f
            o_in = {(c, p): _mm((scores[c, p] * d2_scr[p]).astype(BF16), _stack_heads(v_of(c, p), lo_lane))
                    for c, p in units}
            cross_f = {u: _mm(q_of(*u), s_in[u].astype(BF16)) for u in units}

        if do_mm:
            x1b = _layer_norm(ALPHA * xb_ref[...] + gt_m * zb_scr[...], lng_ref[0:1, :], lnb_ref[0:1, :])
            o_ref[blk_b, :] = x1b
            ff1((x1b * (1.0 + sc_f) + sh_f).astype(BF16), blk_b)

        if do_ret:
            y_scr[:, 0:CONV_DIM + GMLP_DIM] = ycg_ref[...]
            for c, p in units:
                o = o_in[c, p] + wq_scr[0, :, cols(p)] * cross_f[c, p] + wq_scr[1, :, cols(p)] * cross_b[c, p]
                mu = jnp.where(lo_lane,
                               jnp.sum(jnp.where(lo_lane, o, 0.0), axis=-1, keepdims=True),
                               jnp.sum(jnp.where(lo_lane, 0.0, o), axis=-1, keepdims=True)) * inv_d
                d = o - mu
                dd = d * d
                var = jnp.where(lo_lane,
                                jnp.sum(jnp.where(lo_lane, dd, 0.0), axis=-1, keepdims=True),
                                jnp.sum(jnp.where(lo_lane, 0.0, dd), axis=-1, keepdims=True)) * inv_d
                o_n = d * lax.rsqrt(var + LN_EPS)
                y_scr[rows(c), RET_DIM + p * PAIR_W:RET_DIM + (p + 1) * PAIR_W] = (
                    jax.nn.silu(g_ref[rows(c), cols(p)]) * o_n).astype(BF16)

        if do_mm:
            y_ff_a = _mm(hid_scr[blk_a, :], wff2_ref[...])
            y_ff_b = _mm(hid_scr[blk_b, :], wff2_ref[...])
        if do_ret:
            z_a = _mm(y_scr[blk_a, :], wout_ref[...])
            z_b = _mm(y_scr[blk_b, :], wout_ref[...])
        if do_mm:
            o_ref[blk_a, :] = _layer_norm(ALPHA * x1a_scr[...] + gt_f * y_ff_a, lng_ref[1:2, :], lnb_ref[1:2, :])
            o_ref[blk_b, :] = _layer_norm(ALPHA * o_ref[blk_b, :] + gt_f * y_ff_b, lng_ref[1:2, :], lnb_ref[1:2, :])
        if do_ret:
            r_gt_m, r_sh_f, r_sc_f, _ = mods(mod_ret_ref)
            x1a = _layer_norm(ALPHA * xa_ref[...] + r_gt_m * z_a, lng_ref[0:1, :], lnb_ref[0:1, :])
            x1a_scr[...] = x1a
            hfa_scr[...] = (x1a * (1.0 + r_sc_f) + r_sh_f).astype(BF16)
            zb_scr[...] = z_b

    pl.when(s == 0)(lambda: step(True, False))
    pl.when((s > 0) & (s < n_tiles))(lambda: step(True, True))
    pl.when(s == n_tiles)(lambda: step(False, True))


def _run_mix_out(x, ycg, qkv, g, sb, sf0, layer, mod, lng, lnb, wout, wff1, wff2, decf, decb, decf2, decb2, *,
                 tile, mod_row):
    bsz, n, _ = x.shape
    nt = n // tile
    n_tiles = bsz * nt
    n_chunk = tile // CHUNK
    half = tile // 2

    def ret_tile(s):
        i = jnp.minimum(s, n_tiles - 1)
        return lax.div(i, nt), lax.rem(i, nt)

    def mm_tile(s):
        i = jnp.maximum(s - 1, 0)
        return lax.div(i, nt), lax.rem(i, nt)

    def blk_a_of_ret(s):
        b, t = ret_tile(s)
        return b, 2 * t, 0

    def blk_b_of_mm(s):
        b, t = mm_tile(s)
        return b, 2 * t + 1, 0

    ret3 = lambda s: (*ret_tile(s), 0)
    mm3 = lambda s: (*mm_tile(s), 0)
    per_layer = lambda s: (layer, 0, 0)
    per_layer4 = lambda s: (layer, 0, 0, 0)
    if mod_row is None:
        mod_ret_map = lambda s: (layer, ret_tile(s)[0], 0, 0)
        mod_mm_map = lambda s: (layer, mm_tile(s)[0], 0, 0)
    else:
        mod_ret_map = mod_mm_map = lambda s: (layer, mod_row, 0, 0)
    resident = dict(pipeline_mode=pl.Buffered(1))
    return pl.pallas_call(
        functools.partial(_mix_out_kernel, tile=tile, tiles_per_seq=nt, n_tiles=n_tiles),
        grid=(n_tiles + 1,),
        in_specs=[
            pl.BlockSpec((None, half, D_MODEL), blk_a_of_ret),
            pl.BlockSpec((None, half, D_MODEL), blk_b_of_mm),
            pl.BlockSpec((None, tile, CONV_DIM + GMLP_DIM), ret3),
            pl.BlockSpec((None, tile, 3 * RET_DIM), ret3),
            pl.BlockSpec((None, tile, RET_DIM), ret3),
            pl.BlockSpec((None, n_chunk, RET_DIM, PAIR_W), lambda s: (*ret_tile(s), 0, 0)),
            pl.BlockSpec((None, RET_DIM, PAIR_W), lambda s: (ret_tile(s)[0], 0, 0)),
            pl.BlockSpec((None, None, 1, N_MOD * D_MODEL), mod_ret_map),
            pl.BlockSpec((None, None, 1, N_MOD * D_MODEL), mod_mm_map),
            pl.BlockSpec((None, 2, D_MODEL), per_layer),
            pl.BlockSpec((None, 2, D_MODEL), per_layer),
            pl.BlockSpec((None, D_MODEL, D_MODEL), per_layer, **resident),
            pl.BlockSpec((None, D_MODEL, D_FF), per_layer, **resident),
            pl.BlockSpec((None, D_FF, D_MODEL), per_layer, **resident),
            pl.BlockSpec((None, 1, RET_DIM), per_layer),
            pl.BlockSpec((None, 1, RET_DIM), per_layer),
            pl.BlockSpec((None, N_PAIR, 1, 2 * CHUNK), per_layer4),
            pl.BlockSpec((None, N_PAIR, 1, 2 * CHUNK), per_layer4),
        ],
        out_specs=pl.BlockSpec((None, tile, D_MODEL), mm3),
        out_shape=jax.ShapeDtypeStruct((bsz, n, D_MODEL), F32),
        scratch_shapes=[
            pltpu.VMEM((RET_DIM, PAIR_W), F32),
            pltpu.VMEM((N_PAIR, CHUNK, 2 * CHUNK), F32),
            pltpu.VMEM((3, CHUNK, RET_DIM), F32),
            pltpu.VMEM((tile, D_MODEL), BF16),
            pltpu.VMEM((tile, D_FF), BF16),
            pltpu.VMEM((half, D_MODEL), F32),
            pltpu.VMEM((half, D_MODEL), F32),
            pltpu.VMEM((half, D_MODEL), BF16),
        ],
        compiler_params=pltpu.CompilerParams(
            dimension_semantics=("arbitrary",), vmem_limit_bytes=VMEM_LIMIT_BYTES),
        name="mix_out",
    )(x, x, ycg, qkv, g, sb, sf0, mod, mod, lng, lnb, wout, wff1, wff2, decf, decb, decf2, decb2)


def kernel(x, c, ctx, c_ctx, w_ada, b_ada, w_in, conv_w, gmlp_ln_g, gmlp_ln_b, gmlp_ws, gmlp_bs,
           ret_decay_fwd, ret_decay_bwd, w_out, w_ff1, w_ff2, ln_g, ln_b):
    bsz, n_lat, _ = x.shape
    ctx_len = ctx.shape[1]
    lat_tile = 1024
    out_tile = 512
    assert n_lat % lat_tile == 0 and lat_tile % CHUNK == 0 and ctx_len % CHUNK == 0
    assert n_lat % out_tile == 0 and out_tile % (2 * CHUNK) == 0 and ctx_len % (2 * CHUNK) == 0
    assert ctx_len & (ctx_len - 1) == 0 and GRID_W & (GRID_W - 1) == 0

    win_b = w_in.astype(BF16)
    wout_b = w_out.astype(BF16)
    wff1_b = w_ff1.astype(BF16)
    wff2_b = w_ff2.astype(BF16)

    cc = jnp.concatenate([c, c_ctx[None, :], jnp.zeros((MOD_ROWS - bsz - 1, D_MODEL), F32)], axis=0)
    mod = _run_mod(cc, w_ada, b_ada.reshape(DEPTH, 1, N_MOD * D_MODEL))
    mod = mod.reshape(DEPTH, MOD_ROWS, 1, N_MOD * D_MODEL)

    ws = gmlp_ws.astype(BF16)
    ws01 = jnp.concatenate([ws[:, 0], ws[:, 1]], axis=2)
    ws23 = jnp.concatenate([ws[:, 2], ws[:, 3]], axis=2)
    bs_tile = jnp.repeat(jnp.swapaxes(gmlp_bs, 1, 2), GMLP_DIM // GMLP_HEADS, axis=2)
    decf = jnp.repeat(ret_decay_fwd.astype(F32), RET_HEAD_DIM, axis=1)[:, None, :]
    decb = jnp.repeat(ret_decay_bwd.astype(F32), RET_HEAD_DIM, axis=1)[:, None, :]
    decf2 = jnp.repeat(ret_decay_fwd.astype(F32), CHUNK, axis=1).reshape(DEPTH, N_PAIR, 1, 2 * CHUNK)
    decb2 = jnp.repeat(ret_decay_bwd.astype(F32), CHUNK, axis=1).reshape(DEPTH, N_PAIR, 1, 2 * CHUNK)
    mix_consts = (mod, win_b, conv_w, gmlp_ln_g[:, None, :], gmlp_ln_b[:, None, :], ws01, ws23, bs_tile, decf, decb)
    out_consts = (mod, ln_g, ln_b, wout_b, wff1_b, wff2_b, decf, decb, decf2, decb2)

    zero_state = jnp.zeros((bsz, RET_DIM, PAIR_W), F32)
    xc = ctx
    for l in range(DEPTH):
        last = l == DEPTH - 1
        ctx_out = _run_mix_in(xc, l, *mix_consts, zero_state, tile=ctx_len, conv_row=ctx_len,
                              mod_row=CTX_MOD_ROW, emit_states=True, states_only=last)
        sf_ctx, sb_ctx = ctx_out[-2:]
        ycg, qkv, g, sb = _run_mix_in(x, l, *mix_consts, sb_ctx, tile=lat_tile, conv_row=GRID_W,
                                      mod_row=None, emit_states=False)
        x = _run_mix_out(x, ycg, qkv, g, sb, sf_ctx, l, *out_consts, tile=out_tile, mod_row=None)
        if not last:
            ycg_c, qkv_c, g_c, sb_c = ctx_out[:4]
            xc = _run_mix_out(xc, ycg_c, qkv_c, g_c, sb_c, zero_state, l, *out_consts,
                              tile=ctx_len, mod_row=CTX_MOD_ROW)
    return x
```

```python
import functools

import jax
import jax.numpy as jnp
from jax import lax
from jax.experimental import pallas as pl
from jax.experimental.pallas import tpu as pltpu

D_MODEL = 1024
DEPTH = 2
GRID_W = 64
CONV_DIM = 256
GMLP_DIM = 256
GMLP_HEADS = 4
CHUNK = 128
RET_DIM = 512
RET_HEADS = 8
RET_HEAD_DIM = 64
RET_SCALE = RET_HEAD_DIM ** -0.5
D_FF = 4 * D_MODEL
N_MOD = 6
LN_EPS = 1e-5
ALPHA = (2 * DEPTH) ** 0.25
D_IN = 3 * CONV_DIM + 2 * GMLP_DIM + 4 * RET_DIM

C_CONV = 0
C_GMLP = 3 * CONV_DIM
C_Q = C_GMLP + 2 * GMLP_DIM
C_K = C_Q + RET_DIM
C_V = C_K + RET_DIM
C_G = C_V + RET_DIM

N_PAIR = RET_HEADS // 2
PAIR_W = 2 * RET_HEAD_DIM
MOD_ROWS = 16
CTX_MOD_ROW = 8
FF_BLOCK = 1024
SUB_ROWS = 256

VMEM_LIMIT_BYTES = 56 * 1024 * 1024

F32 = jnp.float32
BF16 = jnp.bfloat16


def _layer_norm(v, g, b):
    mu = jnp.mean(v, axis=-1, keepdims=True)
    d = v - mu
    var = jnp.mean(d * d, axis=-1, keepdims=True)
    return d * lax.rsqrt(var + LN_EPS) * g + b


def _mm(a, b):
    return jnp.dot(a, b, preferred_element_type=F32)


def _mm_nt(a, b):
    return lax.dot_general(a, b, (((1,), (1,)), ((), ())), preferred_element_type=F32)


def _mm_tn(a, b):
    return lax.dot_general(a, b, (((0,), (0,)), ((), ())), preferred_element_type=F32)


def _pair_masks():
    lane = lax.broadcasted_iota(jnp.int32, (1, PAIR_W), 1)
    row = lax.broadcasted_iota(jnp.int32, (PAIR_W, 1), 0)
    lo_lane = lane < RET_HEAD_DIM
    block_diag = (row < RET_HEAD_DIM) == lo_lane
    return lo_lane, block_diag


def _stack_heads(a, lo_lane):
    zero = jnp.zeros_like(a)
    return jnp.concatenate([jnp.where(lo_lane, a, zero), jnp.where(lo_lane, zero, a)], axis=0)


def _mod_kernel(cc_ref, w_ref, b_ref, o_ref):
    s = jax.nn.silu(cc_ref[...]).astype(BF16)
    o_ref[...] = _mm(s, w_ref[...].astype(BF16)) + b_ref[...]


def _run_mod(cc, w_ada, b_ada):
    nblk = (N_MOD * D_MODEL) // D_MODEL
    return pl.pallas_call(
        _mod_kernel,
        grid=(DEPTH, nblk),
        in_specs=[
            pl.BlockSpec((MOD_ROWS, D_MODEL), lambda l, j: (0, 0)),
            pl.BlockSpec((None, D_MODEL, D_MODEL), lambda l, j: (l, 0, j)),
            pl.BlockSpec((None, 1, D_MODEL), lambda l, j: (l, 0, j)),
        ],
        out_specs=pl.BlockSpec((None, MOD_ROWS, D_MODEL), lambda l, j: (l, 0, j)),
        out_shape=jax.ShapeDtypeStruct((DEPTH, MOD_ROWS, N_MOD * D_MODEL), F32),
        compiler_params=pltpu.CompilerParams(dimension_semantics=("arbitrary", "arbitrary")),
        name="ada_modulation",
    )(cc, w_ada, b_ada)


def _mix_in_kernel(x_ref, mod_ref, win_ref, convw_ref, glng_ref, glnb_ref, ws01_ref, ws23_ref, bs_ref,
                   decf_ref, decb_ref, sb0_ref, *rest, tile, conv_row, emit_states, states_only, cast_weights):
    if states_only:
        sf_out_ref, sb_out_ref, sb_scr = rest
    elif emit_states:
        ycg_ref, qkv_ref, g_ref, sb_ref, sf_out_ref, sb_out_ref, sb_scr = rest
    elif cast_weights:
        (wout_f32_ref, wff1_f32_ref, wff2_f32_ref, ycg_ref, qkv_ref, g_ref, sb_ref,
         wout_bf_ref, wff1_bf_ref, wff2_bf_ref, sb_scr) = rest
    else:
        ycg_ref, qkv_ref, g_ref, sb_ref, sb_scr = rest
    n_chunk = tile // CHUNK
    t = pl.program_id(1)

    @pl.when(t == 0)
    def _():
        sb_scr[...] = sb0_ref[...]

    sh_m = mod_ref[:, 0 * D_MODEL:1 * D_MODEL]
    sc_m = mod_ref[:, 1 * D_MODEL:2 * D_MODEL]
    h = (x_ref[...] * (1.0 + sc_m) + sh_m).astype(BF16)

    if states_only:
        kv = _mm(h, win_ref[:, C_K:C_K + 2 * RET_DIM])
        k = kv[:, 0:RET_DIM]
        v_bf = kv[:, RET_DIM:2 * RET_DIM].astype(BF16)
    else:
        proj = _mm(h, win_ref[...])

        if cast_weights:
            wout_bf_ref[...] = wout_f32_ref[...].astype(BF16)
            wff1_bf_ref[...] = wff1_f32_ref[...].astype(BF16)
            wff2_bf_ref[...] = wff2_f32_ref[...].astype(BF16)

        xin = proj[:, C_CONV:C_CONV + CONV_DIM]
        gate_b = proj[:, C_CONV + CONV_DIM:C_CONV + 2 * CONV_DIM]
        gate_c = proj[:, C_CONV + 2 * CONV_DIM:C_CONV + 3 * CONV_DIM]
        z = gate_c * xin
        pos_in_row = lax.broadcasted_iota(jnp.int32, (tile, 1), 0) & (conv_row - 1)
        z_prev = jnp.where(pos_in_row == 0, 0.0, pltpu.roll(z, 1, 0))
        z_next = jnp.where(pos_in_row == conv_row - 1, 0.0, pltpu.roll(z, tile - 1, 0))
        y_conv = gate_b * (convw_ref[0:1, :] * z_prev + convw_ref[1:2, :] * z + convw_ref[2:3, :] * z_next)
        ycg_ref[:, 0:CONV_DIM] = y_conv.astype(BF16)

        u_act = jax.nn.gelu(proj[:, C_GMLP:C_GMLP + GMLP_DIM])
        v_ln = _layer_norm(jax.nn.gelu(proj[:, C_GMLP + GMLP_DIM:C_GMLP + 2 * GMLP_DIM]),
                           glng_ref[...], glnb_ref[...]).astype(BF16)
        lane = lax.broadcasted_iota(jnp.int32, (1, GMLP_DIM), 1)
        head_w = GMLP_DIM // GMLP_HEADS
        in_head = [(lane >= hh * head_w) & (lane < (hh + 1) * head_w) for hh in range(GMLP_HEADS)]
        zero_bf = jnp.zeros((CHUNK, GMLP_DIM), BF16)
        for c in range(n_chunk):
            rows = slice(c * CHUNK, (c + 1) * CHUNK)
            vc = v_ln[rows, :]
            by_head = [jnp.where(in_head[hh], vc, zero_bf) for hh in range(GMLP_HEADS)]
            mixed = (_mm(ws01_ref[...], jnp.concatenate(by_head[0:2], axis=0))
                     + _mm(ws23_ref[...], jnp.concatenate(by_head[2:4], axis=0)) + bs_ref[...])
            ycg_ref[rows, CONV_DIM:CONV_DIM + GMLP_DIM] = (u_act[rows, :] * mixed).astype(BF16)

        qkv_ref[:, 0:RET_DIM] = (proj[:, C_Q:C_Q + RET_DIM] * RET_SCALE).astype(BF16)
        k = proj[:, C_K:C_K + RET_DIM]
        qkv_ref[:, RET_DIM:2 * RET_DIM] = k.astype(BF16)
        v_bf = proj[:, C_V:C_V + RET_DIM].astype(BF16)
        qkv_ref[:, 2 * RET_DIM:3 * RET_DIM] = v_bf
        g_ref[...] = proj[:, C_G:C_G + RET_DIM]

    _, block_diag = _pair_masks()
    pos = lax.broadcasted_iota(jnp.int32, (CHUNK, RET_DIM), 0).astype(F32)
    lg_b = -jnp.exp(decb_ref[...])
    w_b = jnp.exp(lg_b * pos)
    decay_chunk_b = jnp.exp(lg_b * float(CHUNK))
    if emit_states:
        lg_f = -jnp.exp(decf_ref[...])
        w_f = jnp.exp(lg_f * (float(CHUNK - 1) - pos))
        decay_chunk_f = jnp.exp(lg_f * float(CHUNK))
    for c in reversed(range(n_chunk)):
        rows = slice(c * CHUNK, (c + 1) * CHUNK)
        for p in range(N_PAIR):
            cols = slice(p * PAIR_W, (p + 1) * PAIR_W)
            s_old = sb_scr[cols, :]
            if not states_only:
                sb_ref[c, cols, :] = s_old.astype(BF16)
            kw = (k[rows, cols] * w_b[:, cols]).astype(BF16)
            upd = _mm_tn(kw, v_bf[rows, cols])
            sb_scr[cols, :] = s_old * decay_chunk_b[:, cols] + jnp.where(block_diag, upd, 0.0)

    if emit_states:
        sb_out_ref[...] = sb_scr[...]
        for p in range(N_PAIR):
            cols = slice(p * PAIR_W, (p + 1) * PAIR_W)
            s_f = jnp.zeros((PAIR_W, PAIR_W), F32)
            for c in range(n_chunk):
                rows = slice(c * CHUNK, (c + 1) * CHUNK)
                kw = (k[rows, cols] * w_f[:, cols]).astype(BF16)
                upd = _mm_tn(kw, v_bf[rows, cols])
                s_f = s_f * decay_chunk_f[:, cols] + jnp.where(block_diag, upd, 0.0)
            sf_out_ref[cols, :] = s_f


def _run_mix_in(x, layer, mod, win, convw, glng, glnb, ws01, ws23, bs_tile, decf, decb, sb0, *,
                tile, conv_row, mod_row, emit_states, states_only=False, mix_out_weights=None):
    bsz, n, _ = x.shape
    nt = n // tile
    n_chunk = tile // CHUNK
    if emit_states:
        assert nt == 1
    assert emit_states or not states_only
    cast_weights = mix_out_weights is not None
    assert not (cast_weights and emit_states)
    rev = lambda b, t: (b, nt - 1 - t, 0)
    per_layer = lambda b, t: (layer, 0, 0)
    mod_map = (lambda b, t: (layer, b, 0, 0)) if mod_row is None else (lambda b, t: (layer, mod_row, 0, 0))
    state_map = lambda b, t: (b, 0, 0)
    out_shape, out_specs = [], []
    if not states_only:
        out_shape += [
            jax.ShapeDtypeStruct((bsz, n, CONV_DIM + GMLP_DIM), BF16),
            jax.ShapeDtypeStruct((bsz, n, 3 * RET_DIM), BF16),
            jax.ShapeDtypeStruct((bsz, n, RET_DIM), F32),
            jax.ShapeDtypeStruct((bsz, n // CHUNK, RET_DIM, PAIR_W), BF16),
        ]
        out_specs += [
            pl.BlockSpec((None, tile, CONV_DIM + GMLP_DIM), rev),
            pl.BlockSpec((None, tile, 3 * RET_DIM), rev),
            pl.BlockSpec((None, tile, RET_DIM), rev),
            pl.BlockSpec((None, n_chunk, RET_DIM, PAIR_W), lambda b, t: (b, nt - 1 - t, 0, 0)),
        ]
    if emit_states:
        out_shape += [jax.ShapeDtypeStruct((bsz, RET_DIM, PAIR_W), F32)] * 2
        out_specs += [pl.BlockSpec((None, RET_DIM, PAIR_W), state_map)] * 2
    extra_in, extra_specs = [], []
    if cast_weights:
        n_steps = bsz * nt
        for w in mix_out_weights:
            slab = w.shape[1] // n_steps
            assert w.shape[1] % n_steps == 0 and slab % 16 == 0
            extra_in.append(w)
            extra_specs.append(pl.BlockSpec((None, slab, w.shape[2]), lambda b, t: (layer, b * nt + t, 0)))
            out_shape.append(jax.ShapeDtypeStruct((1,) + w.shape[1:], BF16))
            out_specs.append(pl.BlockSpec((None, slab, w.shape[2]), lambda b, t: (0, b * nt + t, 0)))
    return pl.pallas_call(
        functools.partial(_mix_in_kernel, tile=tile, conv_row=conv_row, emit_states=emit_states,
                          states_only=states_only, cast_weights=cast_weights),
        grid=(bsz, nt),
        in_specs=[
            pl.BlockSpec((None, tile, D_MODEL), rev),
            pl.BlockSpec((None, None, 1, N_MOD * D_MODEL), mod_map),
            pl.BlockSpec((None, D_MODEL, D_IN), per_layer),
            pl.BlockSpec((None, 3, CONV_DIM), per_layer),
            pl.BlockSpec((None, 1, GMLP_DIM), per_layer),
            pl.BlockSpec((None, 1, GMLP_DIM), per_layer),
            pl.BlockSpec((None, CHUNK, 2 * CHUNK), per_layer),
            pl.BlockSpec((None, CHUNK, 2 * CHUNK), per_layer),
            pl.BlockSpec((None, CHUNK, GMLP_DIM), per_layer),
            pl.BlockSpec((None, 1, RET_DIM), per_layer),
            pl.BlockSpec((None, 1, RET_DIM), per_layer),
            pl.BlockSpec((None, RET_DIM, PAIR_W), state_map),
        ] + extra_specs,
        out_specs=out_specs,
        out_shape=out_shape,
        scratch_shapes=[pltpu.VMEM((RET_DIM, PAIR_W), F32)],
        compiler_params=pltpu.CompilerParams(
            dimension_semantics=("arbitrary", "arbitrary"), vmem_limit_bytes=VMEM_LIMIT_BYTES),
        name="ctx_states" if states_only else ("mix_in_ctx" if emit_states else "mix_in"),
    )(x, mod, win, convw, glng, glnb, ws01, ws23, bs_tile, decf, decb, sb0, *extra_in)


def _mix_out_kernel(x_ref, ycg_ref, qkv_ref, g_ref, sb_ref, sf0_ref, mod_ref, lng_ref, lnb_ref,
                    wout_ref, wff1_ref, wff2_ref, decf_ref, decb_ref, decf2_ref, decb2_ref,
                    o_ref, sf_scr, d2_scr, wq_scr, y_scr, hid_scr, *, tile, sub, tiles_per_seq, n_tiles):
    n_chunk = tile // CHUNK
    s = pl.program_id(0)
    t_ret = lax.rem(jnp.minimum(s, n_tiles - 1), tiles_per_seq)

    lg_f = -jnp.exp(decf_ref[...])
    lg_b = -jnp.exp(decb_ref[...])

    @pl.when(s == 0)
    def _():
        pos = lax.broadcasted_iota(jnp.int32, (CHUNK, RET_DIM), 0).astype(F32)
        wq_scr[0] = jnp.exp(lg_f * (pos + 1.0))
        wq_scr[1] = jnp.exp(lg_b * (float(CHUNK) - pos))
        wq_scr[2] = jnp.exp(lg_f * (float(CHUNK - 1) - pos))
        i = lax.broadcasted_iota(jnp.int32, (CHUNK, 2 * CHUNK), 0)
        j = lax.broadcasted_iota(jnp.int32, (CHUNK, 2 * CHUNK), 1) & (CHUNK - 1)
        rel = i - j
        causal = rel >= 0
        relf = rel.astype(F32)
        for p in range(N_PAIR):
            lf = -jnp.exp(decf2_ref[p])
            lb = -jnp.exp(decb2_ref[p])
            d2_scr[p] = jnp.where(causal, jnp.exp(lf * jnp.where(causal, relf, 0.0)),
                                  jnp.exp(lb * jnp.where(causal, 0.0, -relf)))

    @pl.when(t_ret == 0)
    def _():
        sf_scr[...] = sf0_ref[...]

    lo_lane, block_diag = _pair_masks()
    decay_chunk_f = jnp.exp(lg_f * float(CHUNK))
    inv_d = 1.0 / RET_HEAD_DIM

    gt_m = mod_ref[:, 2 * D_MODEL:3 * D_MODEL]
    sh_f = mod_ref[:, 3 * D_MODEL:4 * D_MODEL]
    sc_f = mod_ref[:, 4 * D_MODEL:5 * D_MODEL]
    gt_f = mod_ref[:, 5 * D_MODEL:6 * D_MODEL]

    units = [(c, p) for c in range(n_chunk) for p in range(N_PAIR)]
    rows = lambda c: slice(c * CHUNK, (c + 1) * CHUNK)
    cols = lambda p: slice(p * PAIR_W, (p + 1) * PAIR_W)
    q_of = lambda c, p: qkv_ref[rows(c), p * PAIR_W:(p + 1) * PAIR_W]
    k_of = lambda c, p: qkv_ref[rows(c), RET_DIM + p * PAIR_W:RET_DIM + (p + 1) * PAIR_W]
    v_of = lambda c, p: qkv_ref[rows(c), 2 * RET_DIM + p * PAIR_W:2 * RET_DIM + (p + 1) * PAIR_W]
    blocks = [slice(i * sub, (i + 1) * sub) for i in range(tile // sub)]

    def step(do_ret, do_mm):
        if do_ret:
            scores = {u: _mm_nt(q_of(*u), _stack_heads(k_of(*u), lo_lane)) for u in units}
            upd = {}
            for c, p in units:
                kw = (k_of(c, p).astype(F32) * wq_scr[2, :, cols(p)]).astype(BF16)
                upd[c, p] = _mm_tn(kw, v_of(c, p))
            cross_b = {(c, p): _mm(q_of(c, p), sb_ref[c, cols(p), :]) for c, p in units}

        if do_mm:
            z = [_mm(y_scr[r, :], wout_ref[...]) for r in blocks]

        if do_ret:
            s_in = {}
            for p in range(N_PAIR):
                s_f = sf_scr[cols(p), :]
                for c in range(n_chunk):
                    s_in[c, p] = s_f
                    s_f = s_f * decay_chunk_f[:, cols(p)] + jnp.where(block_diag, upd[c, p], 0.0)
                sf_scr[cols(p), :] = s_f
            o_in = {(c, p): _mm((scores[c, p] * d2_scr[p]).astype(BF16), _stack_heads(v_of(c, p), lo_lane))
                    for c, p in units}
            cross_f = {u: _mm(q_of(*u), s_in[u].astype(BF16)) for u in units}

        if do_mm:
            for r, z_r in zip(blocks, z):
                x1 = _layer_norm(ALPHA * x_ref[r, :] + gt_m * z_r, lng_ref[0:1, :], lnb_ref[0:1, :])
                o_ref[r, :] = x1
                hf = (x1 * (1.0 + sc_f) + sh_f).astype(BF16)
                for nb in range(D_FF // FF_BLOCK):
                    ff_cols = slice(nb * FF_BLOCK, (nb + 1) * FF_BLOCK)
                    a = jnp.maximum(_mm(hf, wff1_ref[:, ff_cols]), 0.0)
                    hid_scr[r, ff_cols] = (a * a).astype(BF16)

        if do_ret:
            y_scr[:, 0:CONV_DIM + GMLP_DIM] = ycg_ref[...]
            for c, p in units:
                o = o_in[c, p] + wq_scr[0, :, cols(p)] * cross_f[c, p] + wq_scr[1, :, cols(p)] * cross_b[c, p]
                mu = jnp.where(lo_lane,
                               jnp.sum(jnp.where(lo_lane, o, 0.0), axis=-1, keepdims=True),
                               jnp.sum(jnp.where(lo_lane, 0.0, o), axis=-1, keepdims=True)) * inv_d
                d = o - mu
                dd = d * d
                var = jnp.where(lo_lane,
                                jnp.sum(jnp.where(lo_lane, dd, 0.0), axis=-1, keepdims=True),
                                jnp.sum(jnp.where(lo_lane, 0.0, dd), axis=-1, keepdims=True)) * inv_d
                o_n = d * lax.rsqrt(var + LN_EPS)
                y_scr[rows(c), RET_DIM + p * PAIR_W:RET_DIM + (p + 1) * PAIR_W] = (
                    jax.nn.silu(g_ref[rows(c), cols(p)]) * o_n).astype(BF16)

        if do_mm:
            y_ff = [_mm(hid_scr[r, :], wff2_ref[...]) for r in blocks]
            for r, y_r in zip(blocks, y_ff):
                o_ref[r, :] = _layer_norm(ALPHA * o_ref[r, :] + gt_f * y_r, lng_ref[1:2, :], lnb_ref[1:2, :])

    pl.when(s == 0)(lambda: step(True, False))
    pl.when((s > 0) & (s < n_tiles))(lambda: step(True, True))
    pl.when(s == n_tiles)(lambda: step(False, True))


def _run_mix_out(x, ycg, qkv, g, sb, sf0, layer, mod, lng, lnb, wout, wff1, wff2, decf, decb, decf2, decb2, *,
                 tile, mod_row):
    bsz, n, _ = x.shape
    nt = n // tile
    n_tiles = bsz * nt
    n_chunk = tile // CHUNK

    def ret_tile(s):
        i = jnp.minimum(s, n_tiles - 1)
        return lax.div(i, nt), lax.rem(i, nt)

    def mm_tile(s):
        i = jnp.maximum(s - 1, 0)
        return lax.div(i, nt), lax.rem(i, nt)

    ret3 = lambda s: (*ret_tile(s), 0)
    mm3 = lambda s: (*mm_tile(s), 0)
    per_layer = lambda s: (layer, 0, 0)
    per_layer4 = lambda s: (layer, 0, 0, 0)
    this_layer = lambda s: (0, 0, 0)
    mod_map = ((lambda s: (layer, mm_tile(s)[0], 0, 0)) if mod_row is None
               else (lambda s: (layer, mod_row, 0, 0)))
    resident = dict(pipeline_mode=pl.Buffered(1))
    return pl.pallas_call(
        functools.partial(_mix_out_kernel, tile=tile, sub=min(tile, SUB_ROWS), tiles_per_seq=nt, n_tiles=n_tiles),
        grid=(n_tiles + 1,),
        in_specs=[
            pl.BlockSpec((None, tile, D_MODEL), mm3),
            pl.BlockSpec((None, tile, CONV_DIM + GMLP_DIM), ret3),
            pl.BlockSpec((None, tile, 3 * RET_DIM), ret3),
            pl.BlockSpec((None, tile, RET_DIM), ret3),
            pl.BlockSpec((None, n_chunk, RET_DIM, PAIR_W), lambda s: (*ret_tile(s), 0, 0)),
            pl.BlockSpec((None, RET_DIM, PAIR_W), lambda s: (ret_tile(s)[0], 0, 0)),
            pl.BlockSpec((None, None, 1, N_MOD * D_MODEL), mod_map),
            pl.BlockSpec((None, 2, D_MODEL), per_layer),
            pl.BlockSpec((None, 2, D_MODEL), per_layer),
            pl.BlockSpec((None, D_MODEL, D_MODEL), this_layer, **resident),
            pl.BlockSpec((None, D_MODEL, D_FF), this_layer, **resident),
            pl.BlockSpec((None, D_FF, D_MODEL), this_layer, **resident),
            pl.BlockSpec((None, 1, RET_DIM), per_layer),
            pl.BlockSpec((None, 1, RET_DIM), per_layer),
            pl.BlockSpec((None, N_PAIR, 1, 2 * CHUNK), per_layer4),
            pl.BlockSpec((None, N_PAIR, 1, 2 * CHUNK), per_layer4),
        ],
        out_specs=pl.BlockSpec((None, tile, D_MODEL), mm3),
        out_shape=jax.ShapeDtypeStruct((bsz, n, D_MODEL), F32),
        scratch_shapes=[
            pltpu.VMEM((RET_DIM, PAIR_W), F32),
            pltpu.VMEM((N_PAIR, CHUNK, 2 * CHUNK), F32),
            pltpu.VMEM((3, CHUNK, RET_DIM), F32),
            pltpu.VMEM((tile, D_MODEL), BF16),
            pltpu.VMEM((tile, D_FF), BF16),
        ],
        compiler_params=pltpu.CompilerParams(
            dimension_semantics=("arbitrary",), vmem_limit_bytes=VMEM_LIMIT_BYTES),
        name="mix_out",
    )(x, ycg, qkv, g, sb, sf0, mod, lng, lnb, wout, wff1, wff2, decf, decb, decf2, decb2)


def kernel(x, c, ctx, c_ctx, w_ada, b_ada, w_in, conv_w, gmlp_ln_g, gmlp_ln_b, gmlp_ws, gmlp_bs,
           ret_decay_fwd, ret_decay_bwd, w_out, w_ff1, w_ff2, ln_g, ln_b):
    bsz, n_lat, _ = x.shape
    ctx_len = ctx.shape[1]
    lat_tile = 1024
    out_tile = 512
    assert n_lat % lat_tile == 0 and lat_tile % CHUNK == 0 and ctx_len % CHUNK == 0
    assert n_lat % out_tile == 0 and out_tile % SUB_ROWS == 0
    assert ctx_len & (ctx_len - 1) == 0 and GRID_W & (GRID_W - 1) == 0

    win_b = w_in.astype(BF16)

    cc = jnp.concatenate([c, c_ctx[None, :], jnp.zeros((MOD_ROWS - bsz - 1, D_MODEL), F32)], axis=0)
    mod = _run_mod(cc, w_ada, b_ada.reshape(DEPTH, 1, N_MOD * D_MODEL))
    mod = mod.reshape(DEPTH, MOD_ROWS, 1, N_MOD * D_MODEL)

    ws = gmlp_ws.astype(BF16)
    ws01 = jnp.concatenate([ws[:, 0], ws[:, 1]], axis=2)
    ws23 = jnp.concatenate([ws[:, 2], ws[:, 3]], axis=2)
    bs_tile = jnp.repeat(jnp.swapaxes(gmlp_bs, 1, 2), GMLP_DIM // GMLP_HEADS, axis=2)
    decf = jnp.repeat(ret_decay_fwd.astype(F32), RET_HEAD_DIM, axis=1)[:, None, :]
    decb = jnp.repeat(ret_decay_bwd.astype(F32), RET_HEAD_DIM, axis=1)[:, None, :]
    decf2 = jnp.repeat(ret_decay_fwd.astype(F32), CHUNK, axis=1).reshape(DEPTH, N_PAIR, 1, 2 * CHUNK)
    decb2 = jnp.repeat(ret_decay_bwd.astype(F32), CHUNK, axis=1).reshape(DEPTH, N_PAIR, 1, 2 * CHUNK)
    mix_consts = (mod, win_b, conv_w, gmlp_ln_g[:, None, :], gmlp_ln_b[:, None, :], ws01, ws23, bs_tile, decf, decb)
    dec_consts = (decf, decb, decf2, decb2)

    zero_state = jnp.zeros((bsz, RET_DIM, PAIR_W), F32)
    xc = ctx
    for l in range(DEPTH):
        last = l == DEPTH - 1
        ctx_out = _run_mix_in(xc, l, *mix_consts, zero_state, tile=ctx_len, conv_row=ctx_len,
                              mod_row=CTX_MOD_ROW, emit_states=True, states_only=last)
        sf_ctx, sb_ctx = ctx_out[-2:]
        ycg, qkv, g, sb, wout_b, wff1_b, wff2_b = _run_mix_in(
            x, l, *mix_consts, sb_ctx, tile=lat_tile, conv_row=GRID_W, mod_row=None, emit_states=False,
            mix_out_weights=(w_out, w_ff1, w_ff2))
        out_consts = (mod, ln_g, ln_b, wout_b, wff1_b, wff2_b, *dec_consts)
        x = _run_mix_out(x, ycg, qkv, g, sb, sf_ctx, l, *out_consts, tile=out_tile, mod_row=None)
        if not last:
            ycg_c, qkv_c, g_c, sb_c = ctx_out[:4]
            xc = _run_mix_out(xc, ycg_c, qkv_c, g_c, sb_c, zero_state, l, *out_consts,
                              tile=ctx_len, mod_row=CTX_MOD_ROW)
    return x
```

```python
import functools

import jax
import jax.numpy as jnp
from jax import lax
from jax.experimental import pallas as pl
from jax.experimental.pallas import tpu as pltpu

D_MODEL = 1024
DEPTH = 2
GRID_W = 64
CONV_DIM = 256
GMLP_DIM = 256
GMLP_HEADS = 4
CHUNK = 128
RET_DIM = 512
RET_HEADS = 8
RET_HEAD_DIM = 64
RET_SCALE = RET_HEAD_DIM ** -0.5
D_FF = 4 * D_MODEL
N_MOD = 6
LN_EPS = 1e-5
ALPHA = (2 * DEPTH) ** 0.25
D_IN = 3 * CONV_DIM + 2 * GMLP_DIM + 4 * RET_DIM

C_CONV = 0
C_GMLP = 3 * CONV_DIM
C_Q = C_GMLP + 2 * GMLP_DIM
C_K = C_Q + RET_DIM
C_V = C_K + RET_DIM
C_G = C_V + RET_DIM

N_PAIR = RET_HEADS // 2
PAIR_W = 2 * RET_HEAD_DIM
MOD_ROWS = 16
CTX_MOD_ROW = 8
FF_BLOCK = 1024
SUB_ROWS = 256

VMEM_LIMIT_BYTES = 56 * 1024 * 1024

F32 = jnp.float32
BF16 = jnp.bfloat16


def _layer_norm(v, g, b):
    mu = jnp.mean(v, axis=-1, keepdims=True)
    d = v - mu
    var = jnp.mean(d * d, axis=-1, keepdims=True)
    return d * lax.rsqrt(var + LN_EPS) * g + b


def _mm(a, b):
    return jnp.dot(a, b, preferred_element_type=F32)


def _mm_nt(a, b):
    return lax.dot_general(a, b, (((1,), (1,)), ((), ())), preferred_element_type=F32)


def _mm_tn(a, b):
    return lax.dot_general(a, b, (((0,), (0,)), ((), ())), preferred_element_type=F32)


def _pair_masks():
    lane = lax.broadcasted_iota(jnp.int32, (1, PAIR_W), 1)
    row = lax.broadcasted_iota(jnp.int32, (PAIR_W, 1), 0)
    lo_lane = lane < RET_HEAD_DIM
    block_diag = (row < RET_HEAD_DIM) == lo_lane
    return lo_lane, block_diag


def _stack_heads(a, lo_lane):
    zero = jnp.zeros_like(a)
    return jnp.concatenate([jnp.where(lo_lane, a, zero), jnp.where(lo_lane, zero, a)], axis=0)


def _mod_kernel(cc_ref, w_ref, b_ref, o_ref):
    s = jax.nn.silu(cc_ref[...]).astype(BF16)
    o_ref[...] = _mm(s, w_ref[...].astype(BF16)) + b_ref[...]


def _run_mod(cc, w_ada, b_ada):
    blk = 2 * D_MODEL
    nblk = (N_MOD * D_MODEL) // blk
    return pl.pallas_call(
        _mod_kernel,
        grid=(DEPTH, nblk),
        in_specs=[
            pl.BlockSpec((MOD_ROWS, D_MODEL), lambda l, j: (0, 0)),
            pl.BlockSpec((None, D_MODEL, blk), lambda l, j: (l, 0, j)),
            pl.BlockSpec((None, 1, blk), lambda l, j: (l, 0, j)),
        ],
        out_specs=pl.BlockSpec((None, MOD_ROWS, blk), lambda l, j: (l, 0, j)),
        out_shape=jax.ShapeDtypeStruct((DEPTH, MOD_ROWS, N_MOD * D_MODEL), F32),
        compiler_params=pltpu.CompilerParams(
            dimension_semantics=("arbitrary", "arbitrary"), vmem_limit_bytes=VMEM_LIMIT_BYTES),
        name="ada_modulation",
    )(cc, w_ada, b_ada)


def _mix_in_kernel(x_ref, mod_ref, win_ref, convw_ref, glng_ref, glnb_ref, ws01_ref, ws23_ref, bs_ref,
                   decf_ref, decb_ref, sb0_ref, *rest, tile, conv_row, emit_states, states_only, n_cast):
    cast_src = cast_dst = ()
    if states_only:
        sf_out_ref, sb_out_ref, sb_scr = rest
    elif emit_states:
        ycg_ref, qkv_ref, g_ref, sb_ref, sf_out_ref, sb_out_ref, sb_scr = rest
    else:
        cast_src = rest[:n_cast]
        ycg_ref, qkv_ref, g_ref, sb_ref = rest[n_cast:n_cast + 4]
        cast_dst = rest[n_cast + 4:2 * n_cast + 4]
        sb_scr = rest[-1]
    n_chunk = tile // CHUNK
    t = pl.program_id(1)

    @pl.when(t == 0)
    def _():
        sb_scr[...] = sb0_ref[...]

    sh_m = mod_ref[:, 0 * D_MODEL:1 * D_MODEL]
    sc_m = mod_ref[:, 1 * D_MODEL:2 * D_MODEL]
    h = (x_ref[...] * (1.0 + sc_m) + sh_m).astype(BF16)

    if states_only:
        kv = _mm(h, win_ref[:, C_K:C_K + 2 * RET_DIM])
        k = kv[:, 0:RET_DIM]
        v_bf = kv[:, RET_DIM:2 * RET_DIM].astype(BF16)
    else:
        proj = _mm(h, win_ref[...])

        for src_ref, dst_ref in zip(cast_src, cast_dst):
            dst_ref[...] = src_ref[...].astype(BF16)

        xin = proj[:, C_CONV:C_CONV + CONV_DIM]
        gate_b = proj[:, C_CONV + CONV_DIM:C_CONV + 2 * CONV_DIM]
        gate_c = proj[:, C_CONV + 2 * CONV_DIM:C_CONV + 3 * CONV_DIM]
        z = gate_c * xin
        pos_in_row = lax.broadcasted_iota(jnp.int32, (tile, 1), 0) & (conv_row - 1)
        z_prev = jnp.where(pos_in_row == 0, 0.0, pltpu.roll(z, 1, 0))
        z_next = jnp.where(pos_in_row == conv_row - 1, 0.0, pltpu.roll(z, tile - 1, 0))
        y_conv = gate_b * (convw_ref[0:1, :] * z_prev + convw_ref[1:2, :] * z + convw_ref[2:3, :] * z_next)
        ycg_ref[:, 0:CONV_DIM] = y_conv.astype(BF16)

        u_act = jax.nn.gelu(proj[:, C_GMLP:C_GMLP + GMLP_DIM])
        v_ln = _layer_norm(jax.nn.gelu(proj[:, C_GMLP + GMLP_DIM:C_GMLP + 2 * GMLP_DIM]),
                           glng_ref[...], glnb_ref[...]).astype(BF16)
        lane = lax.broadcasted_iota(jnp.int32, (1, GMLP_DIM), 1)
        head_w = GMLP_DIM // GMLP_HEADS
        in_head = [(lane >= hh * head_w) & (lane < (hh + 1) * head_w) for hh in range(GMLP_HEADS)]
        zero_bf = jnp.zeros((CHUNK, GMLP_DIM), BF16)
        for c in range(n_chunk):
            rows = slice(c * CHUNK, (c + 1) * CHUNK)
            vc = v_ln[rows, :]
            by_head = [jnp.where(in_head[hh], vc, zero_bf) for hh in range(GMLP_HEADS)]
            mixed = (_mm(ws01_ref[...], jnp.concatenate(by_head[0:2], axis=0))
                     + _mm(ws23_ref[...], jnp.concatenate(by_head[2:4], axis=0)) + bs_ref[...])
            ycg_ref[rows, CONV_DIM:CONV_DIM + GMLP_DIM] = (u_act[rows, :] * mixed).astype(BF16)

        qkv_ref[:, 0:RET_DIM] = (proj[:, C_Q:C_Q + RET_DIM] * RET_SCALE).astype(BF16)
        k = proj[:, C_K:C_K + RET_DIM]
        qkv_ref[:, RET_DIM:2 * RET_DIM] = k.astype(BF16)
        v_bf = proj[:, C_V:C_V + RET_DIM].astype(BF16)
        qkv_ref[:, 2 * RET_DIM:3 * RET_DIM] = v_bf
        g_ref[...] = proj[:, C_G:C_G + RET_DIM]

    _, block_diag = _pair_masks()
    pos = lax.broadcasted_iota(jnp.int32, (CHUNK, RET_DIM), 0).astype(F32)
    lg_b = -jnp.exp(decb_ref[...])
    w_b = jnp.exp(lg_b * pos)
    decay_chunk_b = jnp.exp(lg_b * float(CHUNK))
    if emit_states:
        lg_f = -jnp.exp(decf_ref[...])
        w_f = jnp.exp(lg_f * (float(CHUNK - 1) - pos))
        decay_chunk_f = jnp.exp(lg_f * float(CHUNK))
    for c in reversed(range(n_chunk)):
        rows = slice(c * CHUNK, (c + 1) * CHUNK)
        for p in range(N_PAIR):
            cols = slice(p * PAIR_W, (p + 1) * PAIR_W)
            s_old = sb_scr[cols, :]
            if not states_only:
                sb_ref[c, cols, :] = s_old.astype(BF16)
            kw = (k[rows, cols] * w_b[:, cols]).astype(BF16)
            upd = _mm_tn(kw, v_bf[rows, cols])
            sb_scr[cols, :] = s_old * decay_chunk_b[:, cols] + jnp.where(block_diag, upd, 0.0)

    if emit_states:
        sb_out_ref[...] = sb_scr[...]
        for p in range(N_PAIR):
            cols = slice(p * PAIR_W, (p + 1) * PAIR_W)
            s_f = jnp.zeros((PAIR_W, PAIR_W), F32)
            for c in range(n_chunk):
                rows = slice(c * CHUNK, (c + 1) * CHUNK)
                kw = (k[rows, cols] * w_f[:, cols]).astype(BF16)
                upd = _mm_tn(kw, v_bf[rows, cols])
                s_f = s_f * decay_chunk_f[:, cols] + jnp.where(block_diag, upd, 0.0)
            sf_out_ref[cols, :] = s_f


def _run_mix_in(x, layer, mod, win, convw, glng, glnb, ws01, ws23, bs_tile, decf, decb, sb0, *,
                tile, conv_row, mod_row, emit_states, states_only=False, cast_next=()):
    bsz, n, _ = x.shape
    nt = n // tile
    n_chunk = tile // CHUNK
    if emit_states:
        assert nt == 1
    assert emit_states or not states_only
    assert not (cast_next and emit_states)
    rev = lambda b, t: (b, nt - 1 - t, 0)
    per_layer = lambda b, t: (layer, 0, 0)
    mod_map = (lambda b, t: (layer, b, 0, 0)) if mod_row is None else (lambda b, t: (layer, mod_row, 0, 0))
    state_map = lambda b, t: (b, 0, 0)
    out_shape, out_specs = [], []
    if not states_only:
        out_shape += [
            jax.ShapeDtypeStruct((bsz, n, CONV_DIM + GMLP_DIM), BF16),
            jax.ShapeDtypeStruct((bsz, n, 3 * RET_DIM), BF16),
            jax.ShapeDtypeStruct((bsz, n, RET_DIM), F32),
            jax.ShapeDtypeStruct((bsz, n // CHUNK, RET_DIM, PAIR_W), BF16),
        ]
        out_specs += [
            pl.BlockSpec((None, tile, CONV_DIM + GMLP_DIM), rev),
            pl.BlockSpec((None, tile, 3 * RET_DIM), rev),
            pl.BlockSpec((None, tile, RET_DIM), rev),
            pl.BlockSpec((None, n_chunk, RET_DIM, PAIR_W), lambda b, t: (b, nt - 1 - t, 0, 0)),
        ]
    if emit_states:
        out_shape += [jax.ShapeDtypeStruct((bsz, RET_DIM, PAIR_W), F32)] * 2
        out_specs += [pl.BlockSpec((None, RET_DIM, PAIR_W), state_map)] * 2
    extra_in, extra_specs = [], []
    n_steps = bsz * nt
    for w, w_layer in cast_next:
        slab = w.shape[1] // n_steps
        assert w.shape[1] % n_steps == 0 and slab % 16 == 0
        extra_in.append(w)
        extra_specs.append(
            pl.BlockSpec((None, slab, w.shape[2]), lambda b, t, w_layer=w_layer: (w_layer, b * nt + t, 0)))
        out_shape.append(jax.ShapeDtypeStruct((1,) + w.shape[1:], BF16))
        out_specs.append(pl.BlockSpec((None, slab, w.shape[2]), lambda b, t: (0, b * nt + t, 0)))
    return pl.pallas_call(
        functools.partial(_mix_in_kernel, tile=tile, conv_row=conv_row, emit_states=emit_states,
                          states_only=states_only, n_cast=len(cast_next)),
        grid=(bsz, nt),
        in_specs=[
            pl.BlockSpec((None, tile, D_MODEL), rev),
            pl.BlockSpec((None, None, 1, N_MOD * D_MODEL), mod_map),
            pl.BlockSpec((None, D_MODEL, D_IN), lambda b, t: (0, 0, 0)),
            pl.BlockSpec((None, 3, CONV_DIM), per_layer),
            pl.BlockSpec((None, 1, GMLP_DIM), per_layer),
            pl.BlockSpec((None, 1, GMLP_DIM), per_layer),
            pl.BlockSpec((None, CHUNK, 2 * CHUNK), per_layer),
            pl.BlockSpec((None, CHUNK, 2 * CHUNK), per_layer),
            pl.BlockSpec((None, CHUNK, GMLP_DIM), per_layer),
            pl.BlockSpec((None, 1, RET_DIM), per_layer),
            pl.BlockSpec((None, 1, RET_DIM), per_layer),
            pl.BlockSpec((None, RET_DIM, PAIR_W), state_map),
        ] + extra_specs,
        out_specs=out_specs,
        out_shape=out_shape,
        scratch_shapes=[pltpu.VMEM((RET_DIM, PAIR_W), F32)],
        compiler_params=pltpu.CompilerParams(
            dimension_semantics=("arbitrary", "arbitrary"), vmem_limit_bytes=VMEM_LIMIT_BYTES),
        name="ctx_states" if states_only else ("mix_in_ctx" if emit_states else "mix_in"),
    )(x, mod, win, convw, glng, glnb, ws01, ws23, bs_tile, decf, decb, sb0, *extra_in)


def _mix_out_kernel(x_ref, ycg_ref, qkv_ref, g_ref, sb_ref, sf0_ref, mod_ref, lng_ref, lnb_ref,
                    wout_ref, wff1_ref, wff2_ref, decf_ref, decb_ref, decf2_ref, decb2_ref,
                    o_ref, sf_scr, d2_scr, wq_scr, y_scr, hid_scr, *, tile, sub, tiles_per_seq, n_tiles):
    n_chunk = tile // CHUNK
    s = pl.program_id(0)
    t_ret = lax.rem(jnp.minimum(s, n_tiles - 1), tiles_per_seq)

    lg_f = -jnp.exp(decf_ref[...])
    lg_b = -jnp.exp(decb_ref[...])

    @pl.when(s == 0)
    def _():
        pos = lax.broadcasted_iota(jnp.int32, (CHUNK, RET_DIM), 0).astype(F32)
        wq_scr[0] = jnp.exp(lg_f * (pos + 1.0))
        wq_scr[1] = jnp.exp(lg_b * (float(CHUNK) - pos))
        wq_scr[2] = jnp.exp(lg_f * (float(CHUNK - 1) - pos))
        i = lax.broadcasted_iota(jnp.int32, (CHUNK, 2 * CHUNK), 0)
        j = lax.broadcasted_iota(jnp.int32, (CHUNK, 2 * CHUNK), 1) & (CHUNK - 1)
        rel = i - j
        causal = rel >= 0
        relf = rel.astype(F32)
        for p in range(N_PAIR):
            lf = -jnp.exp(decf2_ref[p])
            lb = -jnp.exp(decb2_ref[p])
            d2_scr[p] = jnp.where(causal, jnp.exp(lf * jnp.where(causal, relf, 0.0)),
                                  jnp.exp(lb * jnp.where(causal, 0.0, -relf)))

    @pl.when(t_ret == 0)
    def _():
        sf_scr[...] = sf0_ref[...]

    lo_lane, block_diag = _pair_masks()
    decay_chunk_f = jnp.exp(lg_f * float(CHUNK))
    inv_d = 1.0 / RET_HEAD_DIM

    gt_m = mod_ref[:, 2 * D_MODEL:3 * D_MODEL]
    sh_f = mod_ref[:, 3 * D_MODEL:4 * D_MODEL]
    sc_f = mod_ref[:, 4 * D_MODEL:5 * D_MODEL]
    gt_f = mod_ref[:, 5 * D_MODEL:6 * D_MODEL]

    units = [(c, p) for c in range(n_chunk) for p in range(N_PAIR)]
    rows = lambda c: slice(c * CHUNK, (c + 1) * CHUNK)
    cols = lambda p: slice(p * PAIR_W, (p + 1) * PAIR_W)
    q_of = lambda c, p: qkv_ref[rows(c), p * PAIR_W:(p + 1) * PAIR_W]
    k_of = lambda c, p: qkv_ref[rows(c), RET_DIM + p * PAIR_W:RET_DIM + (p + 1) * PAIR_W]
    v_of = lambda c, p: qkv_ref[rows(c), 2 * RET_DIM + p * PAIR_W:2 * RET_DIM + (p + 1) * PAIR_W]
    blocks = [slice(i * sub, (i + 1) * sub) for i in range(tile // sub)]

    def step(do_ret, do_mm):
        if do_ret:
            scores = {u: _mm_nt(q_of(*u), _stack_heads(k_of(*u), lo_lane)) for u in units}
            upd = {}
            for c, p in units:
                kw = (k_of(c, p).astype(F32) * wq_scr[2, :, cols(p)]).astype(BF16)
                upd[c, p] = _mm_tn(kw, v_of(c, p))
            cross_b = {(c, p): _mm(q_of(c, p), sb_ref[c, cols(p), :]) for c, p in units}

        if do_mm:
            z = [_mm(y_scr[r, :], wout_ref[...]) for r in blocks]

        if do_ret:
            s_in = {}
            for p in range(N_PAIR):
                s_f = sf_scr[cols(p), :]
                for c in range(n_chunk):
                    s_in[c, p] = s_f
                    s_f = s_f * decay_chunk_f[:, cols(p)] + jnp.where(block_diag, upd[c, p], 0.0)
                sf_scr[cols(p), :] = s_f
            o_in = {(c, p): _mm((scores[c, p] * d2_scr[p]).astype(BF16), _stack_heads(v_of(c, p), lo_lane))
                    for c, p in units}
            cross_f = {u: _mm(q_of(*u), s_in[u].astype(BF16)) for u in units}

        if do_mm:
            for r, z_r in zip(blocks, z):
                x1 = _layer_norm(ALPHA * x_ref[r, :] + gt_m * z_r, lng_ref[0:1, :], lnb_ref[0:1, :])
                o_ref[r, :] = x1
                hf = (x1 * (1.0 + sc_f) + sh_f).astype(BF16)
                for nb in range(D_FF // FF_BLOCK):
                    ff_cols = slice(nb * FF_BLOCK, (nb + 1) * FF_BLOCK)
                    a = jnp.maximum(_mm(hf, wff1_ref[:, ff_cols]), 0.0)
                    hid_scr[r, ff_cols] = (a * a).astype(BF16)

        if do_ret:
            y_scr[:, 0:CONV_DIM + GMLP_DIM] = ycg_ref[...]
            for c, p in units:
                o = o_in[c, p] + wq_scr[0, :, cols(p)] * cross_f[c, p] + wq_scr[1, :, cols(p)] * cross_b[c, p]
                mu = jnp.where(lo_lane,
                               jnp.sum(jnp.where(lo_lane, o, 0.0), axis=-1, keepdims=True),
                               jnp.sum(jnp.where(lo_lane, 0.0, o), axis=-1, keepdims=True)) * inv_d
                d = o - mu
                dd = d * d
                var = jnp.where(lo_lane,
                                jnp.sum(jnp.where(lo_lane, dd, 0.0), axis=-1, keepdims=True),
                                jnp.sum(jnp.where(lo_lane, 0.0, dd), axis=-1, keepdims=True)) * inv_d
                o_n = d * lax.rsqrt(var + LN_EPS)
                y_scr[rows(c), RET_DIM + p * PAIR_W:RET_DIM + (p + 1) * PAIR_W] = (
                    jax.nn.silu(g_ref[rows(c), cols(p)]) * o_n).astype(BF16)

        if do_mm:
            y_ff = [_mm(hid_scr[r, :], wff2_ref[...]) for r in blocks]
            for r, y_r in zip(blocks, y_ff):
                o_ref[r, :] = _layer_norm(ALPHA * o_ref[r, :] + gt_f * y_r, lng_ref[1:2, :], lnb_ref[1:2, :])

    pl.when(s == 0)(lambda: step(True, False))
    pl.when((s > 0) & (s < n_tiles))(lambda: step(True, True))
    pl.when(s == n_tiles)(lambda: step(False, True))


def _run_mix_out(x, ycg, qkv, g, sb, sf0, layer, mod, lng, lnb, wout, wff1, wff2, decf, decb, decf2, decb2, *,
                 tile, mod_row):
    bsz, n, _ = x.shape
    nt = n // tile
    n_tiles = bsz * nt
    n_chunk = tile // CHUNK

    def ret_tile(s):
        i = jnp.minimum(s, n_tiles - 1)
        return lax.div(i, nt), lax.rem(i, nt)

    def mm_tile(s):
        i = jnp.maximum(s - 1, 0)
        return lax.div(i, nt), lax.rem(i, nt)

    ret3 = lambda s: (*ret_tile(s), 0)
    mm3 = lambda s: (*mm_tile(s), 0)
    per_layer = lambda s: (layer, 0, 0)
    per_layer4 = lambda s: (layer, 0, 0, 0)
    this_layer = lambda s: (0, 0, 0)
    mod_map = ((lambda s: (layer, mm_tile(s)[0], 0, 0)) if mod_row is None
               else (lambda s: (layer, mod_row, 0, 0)))
    resident = dict(pipeline_mode=pl.Buffered(1))
    return pl.pallas_call(
        functools.partial(_mix_out_kernel, tile=tile, sub=min(tile, SUB_ROWS), tiles_per_seq=nt, n_tiles=n_tiles),
        grid=(n_tiles + 1,),
        in_specs=[
            pl.BlockSpec((None, tile, D_MODEL), mm3),
            pl.BlockSpec((None, tile, CONV_DIM + GMLP_DIM), ret3),
            pl.BlockSpec((None, tile, 3 * RET_DIM), ret3),
            pl.BlockSpec((None, tile, RET_DIM), ret3),
            pl.BlockSpec((None, n_chunk, RET_DIM, PAIR_W), lambda s: (*ret_tile(s), 0, 0)),
            pl.BlockSpec((None, RET_DIM, PAIR_W), lambda s: (ret_tile(s)[0], 0, 0)),
            pl.BlockSpec((None, None, 1, N_MOD * D_MODEL), mod_map),
            pl.BlockSpec((None, 2, D_MODEL), per_layer),
            pl.BlockSpec((None, 2, D_MODEL), per_layer),
            pl.BlockSpec((None, D_MODEL, D_MODEL), this_layer, **resident),
            pl.BlockSpec((None, D_MODEL, D_FF), this_layer, **resident),
            pl.BlockSpec((None, D_FF, D_MODEL), this_layer, **resident),
            pl.BlockSpec((None, 1, RET_DIM), per_layer),
            pl.BlockSpec((None, 1, RET_DIM), per_layer),
            pl.BlockSpec((None, N_PAIR, 1, 2 * CHUNK), per_layer4),
            pl.BlockSpec((None, N_PAIR, 1, 2 * CHUNK), per_layer4),
        ],
        out_specs=pl.BlockSpec((None, tile, D_MODEL), mm3),
        out_shape=jax.ShapeDtypeStruct((bsz, n, D_MODEL), F32),
        scratch_shapes=[
            pltpu.VMEM((RET_DIM, PAIR_W), F32),
            pltpu.VMEM((N_PAIR, CHUNK, 2 * CHUNK), F32),
            pltpu.VMEM((3, CHUNK, RET_DIM), F32),
            pltpu.VMEM((tile, D_MODEL), BF16),
            pltpu.VMEM((tile, D_FF), BF16),
        ],
        compiler_params=pltpu.CompilerParams(
            dimension_semantics=("arbitrary",), vmem_limit_bytes=VMEM_LIMIT_BYTES),
        name="mix_out",
    )(x, ycg, qkv, g, sb, sf0, mod, lng, lnb, wout, wff1, wff2, decf, decb, decf2, decb2)


def kernel(x, c, ctx, c_ctx, w_ada, b_ada, w_in, conv_w, gmlp_ln_g, gmlp_ln_b, gmlp_ws, gmlp_bs,
           ret_decay_fwd, ret_decay_bwd, w_out, w_ff1, w_ff2, ln_g, ln_b):
    bsz, n_lat, _ = x.shape
    ctx_len = ctx.shape[1]
    lat_tile = 1024
    out_tile = 512
    assert n_lat % lat_tile == 0 and lat_tile % CHUNK == 0 and ctx_len % CHUNK == 0
    assert n_lat % out_tile == 0 and out_tile % SUB_ROWS == 0
    assert ctx_len & (ctx_len - 1) == 0 and GRID_W & (GRID_W - 1) == 0

    win_b = w_in[0:1].astype(BF16)

    cc = jnp.concatenate([c, c_ctx[None, :], jnp.zeros((MOD_ROWS - bsz - 1, D_MODEL), F32)], axis=0)
    mod = _run_mod(cc, w_ada, b_ada.reshape(DEPTH, 1, N_MOD * D_MODEL))
    mod = mod.reshape(DEPTH, MOD_ROWS, 1, N_MOD * D_MODEL)

    ws = gmlp_ws.astype(BF16)
    ws01 = jnp.concatenate([ws[:, 0], ws[:, 1]], axis=2)
    ws23 = jnp.concatenate([ws[:, 2], ws[:, 3]], axis=2)
    bs_tile = jnp.repeat(jnp.swapaxes(gmlp_bs, 1, 2), GMLP_DIM // GMLP_HEADS, axis=2)
    decf = jnp.repeat(ret_decay_fwd.astype(F32), RET_HEAD_DIM, axis=1)[:, None, :]
    decb = jnp.repeat(ret_decay_bwd.astype(F32), RET_HEAD_DIM, axis=1)[:, None, :]
    decf2 = jnp.repeat(ret_decay_fwd.astype(F32), CHUNK, axis=1).reshape(DEPTH, N_PAIR, 1, 2 * CHUNK)
    decb2 = jnp.repeat(ret_decay_bwd.astype(F32), CHUNK, axis=1).reshape(DEPTH, N_PAIR, 1, 2 * CHUNK)
    mix_tables = (conv_w, gmlp_ln_g[:, None, :], gmlp_ln_b[:, None, :], ws01, ws23, bs_tile, decf, decb)
    dec_consts = (decf, decb, decf2, decb2)

    zero_state = jnp.zeros((bsz, RET_DIM, PAIR_W), F32)
    xc = ctx
    for l in range(DEPTH):
        last = l == DEPTH - 1
        mix_consts = (mod, win_b, *mix_tables)
        ctx_out = _run_mix_in(xc, l, *mix_consts, zero_state, tile=ctx_len, conv_row=ctx_len,
                              mod_row=CTX_MOD_ROW, emit_states=True, states_only=last)
        sf_ctx, sb_ctx = ctx_out[-2:]
        to_cast = ((w_out, l), (w_ff1, l), (w_ff2, l)) + (() if last else ((w_in, l + 1),))
        ycg, qkv, g, sb, wout_b, wff1_b, wff2_b, *win_next = _run_mix_in(
            x, l, *mix_consts, sb_ctx, tile=lat_tile, conv_row=GRID_W, mod_row=None, emit_states=False,
            cast_next=to_cast)
        out_consts = (mod, ln_g, ln_b, wout_b, wff1_b, wff2_b, *dec_consts)
        x = _run_mix_out(x, ycg, qkv, g, sb, sf_ctx, l, *out_consts, tile=out_tile, mod_row=None)
        if not last:
            ycg_c, qkv_c, g_c, sb_c = ctx_out[:4]
            xc = _run_mix_out(xc, ycg_c, qkv_c, g_c, sb_c, zero_state, l, *out_consts,
                              tile=ctx_len, mod_row=CTX_MOD_ROW)
            (win_b,) = win_next
    return x
```

```python
import functools

import jax
import jax.numpy as jnp
from jax import lax
from jax.experimental import pallas as pl
from jax.experimental.pallas import tpu as pltpu

D_MODEL = 1024
DEPTH = 2
GRID_W = 64
CONV_DIM = 256
GMLP_DIM = 256
GMLP_HEADS = 4
CHUNK = 128
RET_DIM = 512
RET_HEADS = 8
RET_HEAD_DIM = 64
RET_SCALE = RET_HEAD_DIM ** -0.5
D_FF = 4 * D_MODEL
N_MOD = 6
LN_EPS = 1e-5
ALPHA = (2 * DEPTH) ** 0.25
D_IN = 3 * CONV_DIM + 2 * GMLP_DIM + 4 * RET_DIM

C_CONV = 0
C_GMLP = 3 * CONV_DIM
C_Q = C_GMLP + 2 * GMLP_DIM
C_K = C_Q + RET_DIM
C_V = C_K + RET_DIM
C_G = C_V + RET_DIM

N_PAIR = RET_HEADS // 2
PAIR_W = 2 * RET_HEAD_DIM
MOD_ROWS = 16
CTX_MOD_ROW = 8
FF_BLOCK = 1024
SUB_ROWS = 256
MIX_IN_TILE = 1024
MIX_OUT_TILE = 512
BF16_SUBLANES = 16

VMEM_LIMIT_BYTES = 56 * 1024 * 1024

F32 = jnp.float32
BF16 = jnp.bfloat16


def _layer_norm(v, g, b):
    mu = jnp.mean(v, axis=-1, keepdims=True)
    d = v - mu
    var = jnp.mean(d * d, axis=-1, keepdims=True)
    return d * lax.rsqrt(var + LN_EPS) * g + b


def _mm(a, b):
    return jnp.dot(a, b, preferred_element_type=F32)


def _mm_nt(a, b):
    return lax.dot_general(a, b, (((1,), (1,)), ((), ())), preferred_element_type=F32)


def _mm_tn(a, b):
    return lax.dot_general(a, b, (((0,), (0,)), ((), ())), preferred_element_type=F32)


def _pair_masks():
    lane = lax.broadcasted_iota(jnp.int32, (1, PAIR_W), 1)
    row = lax.broadcasted_iota(jnp.int32, (PAIR_W, 1), 0)
    lo_lane = lane < RET_HEAD_DIM
    block_diag = (row < RET_HEAD_DIM) == lo_lane
    return lo_lane, block_diag


def _stack_heads(a, lo_lane):
    zero = jnp.zeros_like(a)
    return jnp.concatenate([jnp.where(lo_lane, a, zero), jnp.where(lo_lane, zero, a)], axis=0)


def _mod_kernel(cc_ref, w_ref, b_ref, o_ref):
    s = jax.nn.silu(cc_ref[...]).astype(BF16)
    o_ref[...] = _mm(s, w_ref[...].astype(BF16)) + b_ref[...]


def _run_mod(cc, w_ada, b_ada):
    blk = 2 * D_MODEL
    nblk = (N_MOD * D_MODEL) // blk
    return pl.pallas_call(
        _mod_kernel,
        grid=(DEPTH, nblk),
        in_specs=[
            pl.BlockSpec((MOD_ROWS, D_MODEL), lambda l, j: (0, 0)),
            pl.BlockSpec((None, D_MODEL, blk), lambda l, j: (l, 0, j)),
            pl.BlockSpec((None, 1, blk), lambda l, j: (l, 0, j)),
        ],
        out_specs=pl.BlockSpec((None, MOD_ROWS, blk), lambda l, j: (l, 0, j)),
        out_shape=jax.ShapeDtypeStruct((DEPTH, MOD_ROWS, N_MOD * D_MODEL), F32),
        compiler_params=pltpu.CompilerParams(
            dimension_semantics=("arbitrary", "arbitrary"), vmem_limit_bytes=VMEM_LIMIT_BYTES),
        name="ada_modulation",
    )(cc, w_ada, b_ada)


def _mix_in_kernel(x_ref, mod_ref, win_ref, convw_ref, glng_ref, glnb_ref, ws01_ref, ws23_ref, bs_ref,
                   decf_ref, decb_ref, sb0_ref, *rest, tile, conv_row, emit_states, states_only, n_cast):
    cast_src = cast_dst = ()
    if states_only:
        sf_out_ref, sb_out_ref, sb_scr = rest
    elif emit_states:
        ycg_ref, qkv_ref, g_ref, sb_ref, sf_out_ref, sb_out_ref, sb_scr = rest
    else:
        cast_src = rest[:n_cast]
        ycg_ref, qkv_ref, g_ref, sb_ref = rest[n_cast:n_cast + 4]
        cast_dst = rest[n_cast + 4:2 * n_cast + 4]
        sb_scr = rest[-1]
    n_chunk = tile // CHUNK
    t = pl.program_id(1)

    @pl.when(t == 0)
    def _():
        sb_scr[...] = sb0_ref[...]

    sh_m = mod_ref[:, 0 * D_MODEL:1 * D_MODEL]
    sc_m = mod_ref[:, 1 * D_MODEL:2 * D_MODEL]
    h = (x_ref[...] * (1.0 + sc_m) + sh_m).astype(BF16)

    if states_only:
        kv = _mm(h, win_ref[:, C_K:C_K + 2 * RET_DIM])
        k = kv[:, 0:RET_DIM]
        v_bf = kv[:, RET_DIM:2 * RET_DIM].astype(BF16)
    else:
        proj = _mm(h, win_ref[...])

        for src_ref, dst_ref in zip(cast_src, cast_dst):
            dst_ref[...] = src_ref[...].astype(BF16)

        xin = proj[:, C_CONV:C_CONV + CONV_DIM]
        gate_b = proj[:, C_CONV + CONV_DIM:C_CONV + 2 * CONV_DIM]
        gate_c = proj[:, C_CONV + 2 * CONV_DIM:C_CONV + 3 * CONV_DIM]
        z = gate_c * xin
        pos_in_row = lax.broadcasted_iota(jnp.int32, (tile, 1), 0) & (conv_row - 1)
        z_prev = jnp.where(pos_in_row == 0, 0.0, pltpu.roll(z, 1, 0))
        z_next = jnp.where(pos_in_row == conv_row - 1, 0.0, pltpu.roll(z, tile - 1, 0))
        y_conv = gate_b * (convw_ref[0:1, :] * z_prev + convw_ref[1:2, :] * z + convw_ref[2:3, :] * z_next)
        ycg_ref[:, 0:CONV_DIM] = y_conv.astype(BF16)

        u_act = jax.nn.gelu(proj[:, C_GMLP:C_GMLP + GMLP_DIM])
        v_ln = _layer_norm(jax.nn.gelu(proj[:, C_GMLP + GMLP_DIM:C_GMLP + 2 * GMLP_DIM]),
                           glng_ref[...], glnb_ref[...]).astype(BF16)
        lane = lax.broadcasted_iota(jnp.int32, (1, GMLP_DIM), 1)
        head_w = GMLP_DIM // GMLP_HEADS
        in_head = [(lane >= hh * head_w) & (lane < (hh + 1) * head_w) for hh in range(GMLP_HEADS)]
        zero_bf = jnp.zeros((CHUNK, GMLP_DIM), BF16)
        for c in range(n_chunk):
            rows = slice(c * CHUNK, (c + 1) * CHUNK)
            vc = v_ln[rows, :]
            by_head = [jnp.where(in_head[hh], vc, zero_bf) for hh in range(GMLP_HEADS)]
            mixed = (_mm(ws01_ref[...], jnp.concatenate(by_head[0:2], axis=0))
                     + _mm(ws23_ref[...], jnp.concatenate(by_head[2:4], axis=0)) + bs_ref[...])
            ycg_ref[rows, CONV_DIM:CONV_DIM + GMLP_DIM] = (u_act[rows, :] * mixed).astype(BF16)

        qkv_ref[:, 0:RET_DIM] = (proj[:, C_Q:C_Q + RET_DIM] * RET_SCALE).astype(BF16)
        k = proj[:, C_K:C_K + RET_DIM]
        qkv_ref[:, RET_DIM:2 * RET_DIM] = k.astype(BF16)
        v_bf = proj[:, C_V:C_V + RET_DIM].astype(BF16)
        qkv_ref[:, 2 * RET_DIM:3 * RET_DIM] = v_bf
        g_ref[...] = proj[:, C_G:C_G + RET_DIM].astype(BF16)

    _, block_diag = _pair_masks()
    pos = lax.broadcasted_iota(jnp.int32, (CHUNK, RET_DIM), 0).astype(F32)
    lg_b = -jnp.exp(decb_ref[...])
    w_b = jnp.exp(lg_b * pos)
    decay_chunk_b = jnp.exp(lg_b * float(CHUNK))
    if emit_states:
        lg_f = -jnp.exp(decf_ref[...])
        w_f = jnp.exp(lg_f * (float(CHUNK - 1) - pos))
        decay_chunk_f = jnp.exp(lg_f * float(CHUNK))
    for c in reversed(range(n_chunk)):
        rows = slice(c * CHUNK, (c + 1) * CHUNK)
        for p in range(N_PAIR):
            cols = slice(p * PAIR_W, (p + 1) * PAIR_W)
            s_old = sb_scr[cols, :]
            if not states_only:
                sb_ref[c, cols, :] = s_old.astype(BF16)
            kw = (k[rows, cols] * w_b[:, cols]).astype(BF16)
            upd = _mm_tn(kw, v_bf[rows, cols])
            sb_scr[cols, :] = s_old * decay_chunk_b[:, cols] + jnp.where(block_diag, upd, 0.0)

    if emit_states:
        sb_out_ref[...] = sb_scr[...]
        for p in range(N_PAIR):
            cols = slice(p * PAIR_W, (p + 1) * PAIR_W)
            s_f = jnp.zeros((PAIR_W, PAIR_W), F32)
            for c in range(n_chunk):
                rows = slice(c * CHUNK, (c + 1) * CHUNK)
                kw = (k[rows, cols] * w_f[:, cols]).astype(BF16)
                upd = _mm_tn(kw, v_bf[rows, cols])
                s_f = s_f * decay_chunk_f[:, cols] + jnp.where(block_diag, upd, 0.0)
            sf_out_ref[cols, :] = s_f


def _run_mix_in(x, layer, mod, win, convw, glng, glnb, ws01, ws23, bs_tile, decf, decb, sb0, *,
                tile, conv_row, mod_row, emit_states, states_only=False, cast_next=()):
    bsz, n, _ = x.shape
    nt = n // tile
    n_chunk = tile // CHUNK
    if emit_states:
        assert nt == 1
    assert emit_states or not states_only
    assert not (cast_next and emit_states)
    rev = lambda b, t: (b, nt - 1 - t, 0)
    per_layer = lambda b, t: (layer, 0, 0)
    mod_map = (lambda b, t: (layer, b, 0, 0)) if mod_row is None else (lambda b, t: (layer, mod_row, 0, 0))
    state_map = lambda b, t: (b, 0, 0)
    out_shape, out_specs = [], []
    if not states_only:
        out_shape += [
            jax.ShapeDtypeStruct((bsz, n, CONV_DIM + GMLP_DIM), BF16),
            jax.ShapeDtypeStruct((bsz, n, 3 * RET_DIM), BF16),
            jax.ShapeDtypeStruct((bsz, n, RET_DIM), BF16),
            jax.ShapeDtypeStruct((bsz, n // CHUNK, RET_DIM, PAIR_W), BF16),
        ]
        out_specs += [
            pl.BlockSpec((None, tile, CONV_DIM + GMLP_DIM), rev),
            pl.BlockSpec((None, tile, 3 * RET_DIM), rev),
            pl.BlockSpec((None, tile, RET_DIM), rev),
            pl.BlockSpec((None, n_chunk, RET_DIM, PAIR_W), lambda b, t: (b, nt - 1 - t, 0, 0)),
        ]
    if emit_states:
        out_shape += [jax.ShapeDtypeStruct((bsz, RET_DIM, PAIR_W), F32)] * 2
        out_specs += [pl.BlockSpec((None, RET_DIM, PAIR_W), state_map)] * 2
    extra_in, extra_specs = [], []
    n_steps = bsz * nt
    for w, w_layer in cast_next:
        slab = w.shape[1] // n_steps
        assert w.shape[1] % n_steps == 0 and slab % BF16_SUBLANES == 0
        extra_in.append(w)
        extra_specs.append(
            pl.BlockSpec((None, slab, w.shape[2]), lambda b, t, w_layer=w_layer: (w_layer, b * nt + t, 0)))
        out_shape.append(jax.ShapeDtypeStruct((1,) + w.shape[1:], BF16))
        out_specs.append(pl.BlockSpec((None, slab, w.shape[2]), lambda b, t: (0, b * nt + t, 0)))
    return pl.pallas_call(
        functools.partial(_mix_in_kernel, tile=tile, conv_row=conv_row, emit_states=emit_states,
                          states_only=states_only, n_cast=len(cast_next)),
        grid=(bsz, nt),
        in_specs=[
            pl.BlockSpec((None, tile, D_MODEL), rev),
            pl.BlockSpec((None, None, 1, N_MOD * D_MODEL), mod_map),
            pl.BlockSpec((None, D_MODEL, D_IN), lambda b, t: (0, 0, 0)),
            pl.BlockSpec((None, 3, CONV_DIM), per_layer),
            pl.BlockSpec((None, 1, GMLP_DIM), per_layer),
            pl.BlockSpec((None, 1, GMLP_DIM), per_layer),
            pl.BlockSpec((None, CHUNK, 2 * CHUNK), per_layer),
            pl.BlockSpec((None, CHUNK, 2 * CHUNK), per_layer),
            pl.BlockSpec((None, CHUNK, GMLP_DIM), per_layer),
            pl.BlockSpec((None, 1, RET_DIM), per_layer),
            pl.BlockSpec((None, 1, RET_DIM), per_layer),
            pl.BlockSpec((None, RET_DIM, PAIR_W), state_map),
        ] + extra_specs,
        out_specs=out_specs,
        out_shape=out_shape,
        scratch_shapes=[pltpu.VMEM((RET_DIM, PAIR_W), F32)],
        compiler_params=pltpu.CompilerParams(
            dimension_semantics=("arbitrary", "arbitrary"), vmem_limit_bytes=VMEM_LIMIT_BYTES),
        name="ctx_states" if states_only else ("mix_in_ctx" if emit_states else "mix_in"),
    )(x, mod, win, convw, glng, glnb, ws01, ws23, bs_tile, decf, decb, sb0, *extra_in)


def _mix_out_kernel(x_ref, ycg_ref, qkv_ref, g_ref, sb_ref, sf0_ref, mod_ref, lng_ref, lnb_ref,
                    wout_ref, wff1_ref, wff2_ref, decf_ref, decb_ref, decf2_ref, decb2_ref,
                    o_ref, sf_scr, d2_scr, wq_scr, y_scr, hid_scr, *, tile, sub, tiles_per_seq, n_tiles):
    n_chunk = tile // CHUNK
    s = pl.program_id(0)
    t_ret = lax.rem(jnp.minimum(s, n_tiles - 1), tiles_per_seq)

    lg_f = -jnp.exp(decf_ref[...])
    lg_b = -jnp.exp(decb_ref[...])

    @pl.when(s == 0)
    def _():
        pos = lax.broadcasted_iota(jnp.int32, (CHUNK, RET_DIM), 0).astype(F32)
        wq_scr[0] = jnp.exp(lg_f * (pos + 1.0))
        wq_scr[1] = jnp.exp(lg_b * (float(CHUNK) - pos))
        wq_scr[2] = jnp.exp(lg_f * (float(CHUNK - 1) - pos))
        i = lax.broadcasted_iota(jnp.int32, (CHUNK, 2 * CHUNK), 0)
        j = lax.broadcasted_iota(jnp.int32, (CHUNK, 2 * CHUNK), 1) & (CHUNK - 1)
        rel = i - j
        causal = rel >= 0
        relf = rel.astype(F32)
        for p in range(N_PAIR):
            lf = -jnp.exp(decf2_ref[p])
            lb = -jnp.exp(decb2_ref[p])
            d2_scr[p] = jnp.where(causal, jnp.exp(lf * jnp.where(causal, relf, 0.0)),
                                  jnp.exp(lb * jnp.where(causal, 0.0, -relf)))

    @pl.when(t_ret == 0)
    def _():
        sf_scr[...] = sf0_ref[...]

    lo_lane, block_diag = _pair_masks()
    decay_chunk_f = jnp.exp(lg_f * float(CHUNK))
    inv_d = 1.0 / RET_HEAD_DIM

    gt_m = mod_ref[:, 2 * D_MODEL:3 * D_MODEL]
    sh_f = mod_ref[:, 3 * D_MODEL:4 * D_MODEL]
    sc_f = mod_ref[:, 4 * D_MODEL:5 * D_MODEL]
    gt_f = mod_ref[:, 5 * D_MODEL:6 * D_MODEL]

    units = [(c, p) for c in range(n_chunk) for p in range(N_PAIR)]
    rows = lambda c: slice(c * CHUNK, (c + 1) * CHUNK)
    cols = lambda p: slice(p * PAIR_W, (p + 1) * PAIR_W)
    q_of = lambda c, p: qkv_ref[rows(c), p * PAIR_W:(p + 1) * PAIR_W]
    k_of = lambda c, p: qkv_ref[rows(c), RET_DIM + p * PAIR_W:RET_DIM + (p + 1) * PAIR_W]
    v_of = lambda c, p: qkv_ref[rows(c), 2 * RET_DIM + p * PAIR_W:2 * RET_DIM + (p + 1) * PAIR_W]
    blocks = [slice(i * sub, (i + 1) * sub) for i in range(tile // sub)]

    def step(do_ret, do_mm):
        if do_ret:
            scores = {u: _mm_nt(q_of(*u), _stack_heads(k_of(*u), lo_lane)) for u in units}
            upd = {}
            for c, p in units:
                kw = (k_of(c, p).astype(F32) * wq_scr[2, :, cols(p)]).astype(BF16)
                upd[c, p] = _mm_tn(kw, v_of(c, p))
            cross_b = {(c, p): _mm(q_of(c, p), sb_ref[c, cols(p), :]) for c, p in units}

        if do_mm:
            z = [_mm(y_scr[r, :], wout_ref[...]) for r in blocks]

        if do_ret:
            s_in = {}
            for p in range(N_PAIR):
                s_f = sf_scr[cols(p), :]
                for c in range(n_chunk):
                    s_in[c, p] = s_f
                    s_f = s_f * decay_chunk_f[:, cols(p)] + jnp.where(block_diag, upd[c, p], 0.0)
                sf_scr[cols(p), :] = s_f
            o_in = {(c, p): _mm((scores[c, p] * d2_scr[p]).astype(BF16), _stack_heads(v_of(c, p), lo_lane))
                    for c, p in units}
            cross_f = {u: _mm(q_of(*u), s_in[u].astype(BF16)) for u in units}

        if do_mm:
            for r, z_r in zip(blocks, z):
                x1 = _layer_norm(ALPHA * x_ref[r, :] + gt_m * z_r, lng_ref[0:1, :], lnb_ref[0:1, :])
                o_ref[r, :] = x1
                hf = (x1 * (1.0 + sc_f) + sh_f).astype(BF16)
                for nb in range(D_FF // FF_BLOCK):
                    ff_cols = slice(nb * FF_BLOCK, (nb + 1) * FF_BLOCK)
                    a = jnp.maximum(_mm(hf, wff1_ref[:, ff_cols]), 0.0)
                    hid_scr[r, ff_cols] = (a * a).astype(BF16)

        if do_ret:
            y_scr[:, 0:CONV_DIM + GMLP_DIM] = ycg_ref[...]
            for c, p in units:
                o = o_in[c, p] + wq_scr[0, :, cols(p)] * cross_f[c, p] + wq_scr[1, :, cols(p)] * cross_b[c, p]
                mu = jnp.where(lo_lane,
                               jnp.sum(jnp.where(lo_lane, o, 0.0), axis=-1, keepdims=True),
                               jnp.sum(jnp.where(lo_lane, 0.0, o), axis=-1, keepdims=True)) * inv_d
                d = o - mu
                dd = d * d
                var = jnp.where(lo_lane,
                                jnp.sum(jnp.where(lo_lane, dd, 0.0), axis=-1, keepdims=True),
                                jnp.sum(jnp.where(lo_lane, 0.0, dd), axis=-1, keepdims=True)) * inv_d
                o_n = d * lax.rsqrt(var + LN_EPS)
                y_scr[rows(c), RET_DIM + p * PAIR_W:RET_DIM + (p + 1) * PAIR_W] = (
                    jax.nn.silu(g_ref[rows(c), cols(p)].astype(F32)) * o_n).astype(BF16)

        if do_mm:
            y_ff = [_mm(hid_scr[r, :], wff2_ref[...]) for r in blocks]
            for r, y_r in zip(blocks, y_ff):
                o_ref[r, :] = _layer_norm(ALPHA * o_ref[r, :] + gt_f * y_r, lng_ref[1:2, :], lnb_ref[1:2, :])

    pl.when(s == 0)(lambda: step(True, False))
    pl.when((s > 0) & (s < n_tiles))(lambda: step(True, True))
    pl.when(s == n_tiles)(lambda: step(False, True))


def _run_mix_out(x, ycg, qkv, g, sb, sf0, layer, mod, lng, lnb, wout, wff1, wff2, decf, decb, decf2, decb2, *,
                 tile, mod_row):
    bsz, n, _ = x.shape
    nt = n // tile
    n_tiles = bsz * nt
    n_chunk = tile // CHUNK

    def ret_tile(s):
        i = jnp.minimum(s, n_tiles - 1)
        return lax.div(i, nt), lax.rem(i, nt)

    def mm_tile(s):
        i = jnp.maximum(s - 1, 0)
        return lax.div(i, nt), lax.rem(i, nt)

    ret3 = lambda s: (*ret_tile(s), 0)
    mm3 = lambda s: (*mm_tile(s), 0)
    per_layer = lambda s: (layer, 0, 0)
    per_layer4 = lambda s: (layer, 0, 0, 0)
    this_layer = lambda s: (0, 0, 0)
    mod_map = ((lambda s: (layer, mm_tile(s)[0], 0, 0)) if mod_row is None
               else (lambda s: (layer, mod_row, 0, 0)))
    resident = dict(pipeline_mode=pl.Buffered(1))
    return pl.pallas_call(
        functools.partial(_mix_out_kernel, tile=tile, sub=min(tile // 2, SUB_ROWS), tiles_per_seq=nt, n_tiles=n_tiles),
        grid=(n_tiles + 1,),
        in_specs=[
            pl.BlockSpec((None, tile, D_MODEL), mm3),
            pl.BlockSpec((None, tile, CONV_DIM + GMLP_DIM), ret3),
            pl.BlockSpec((None, tile, 3 * RET_DIM), ret3),
            pl.BlockSpec((None, tile, RET_DIM), ret3),
            pl.BlockSpec((None, n_chunk, RET_DIM, PAIR_W), lambda s: (*ret_tile(s), 0, 0)),
            pl.BlockSpec((None, RET_DIM, PAIR_W), lambda s: (ret_tile(s)[0], 0, 0)),
            pl.BlockSpec((None, None, 1, N_MOD * D_MODEL), mod_map),
            pl.BlockSpec((None, 2, D_MODEL), per_layer),
            pl.BlockSpec((None, 2, D_MODEL), per_layer),
            pl.BlockSpec((None, D_MODEL, D_MODEL), this_layer, **resident),
            pl.BlockSpec((None, D_MODEL, D_FF), this_layer, **resident),
            pl.BlockSpec((None, D_FF, D_MODEL), this_layer, **resident),
            pl.BlockSpec((None, 1, RET_DIM), per_layer),
            pl.BlockSpec((None, 1, RET_DIM), per_layer),
            pl.BlockSpec((None, N_PAIR, 1, 2 * CHUNK), per_layer4),
            pl.BlockSpec((None, N_PAIR, 1, 2 * CHUNK), per_layer4),
        ],
        out_specs=pl.BlockSpec((None, tile, D_MODEL), mm3),
        out_shape=jax.ShapeDtypeStruct((bsz, n, D_MODEL), F32),
        scratch_shapes=[
            pltpu.VMEM((RET_DIM, PAIR_W), F32),
            pltpu.VMEM((N_PAIR, CHUNK, 2 * CHUNK), F32),
            pltpu.VMEM((3, CHUNK, RET_DIM), F32),
            pltpu.VMEM((tile, D_MODEL), BF16),
            pltpu.VMEM((tile, D_FF), BF16),
        ],
        compiler_params=pltpu.CompilerParams(
            dimension_semantics=("arbitrary",), vmem_limit_bytes=VMEM_LIMIT_BYTES),
        name="mix_out",
    )(x, ycg, qkv, g, sb, sf0, mod, lng, lnb, wout, wff1, wff2, decf, decb, decf2, decb2)


def kernel(x, c, ctx, c_ctx, w_ada, b_ada, w_in, conv_w, gmlp_ln_g, gmlp_ln_b, gmlp_ws, gmlp_bs,
           ret_decay_fwd, ret_decay_bwd, w_out, w_ff1, w_ff2, ln_g, ln_b):
    bsz, n_lat, _ = x.shape
    ctx_len = ctx.shape[1]
    assert n_lat % MIX_IN_TILE == 0 and MIX_IN_TILE % CHUNK == 0 and ctx_len % (2 * CHUNK) == 0
    assert n_lat % MIX_OUT_TILE == 0 and MIX_OUT_TILE % (2 * CHUNK) == 0
    assert ctx_len & (ctx_len - 1) == 0 and GRID_W & (GRID_W - 1) == 0

    win_b = w_in[0:1].astype(BF16)

    cc = jnp.concatenate([c, c_ctx[None, :], jnp.zeros((MOD_ROWS - bsz - 1, D_MODEL), F32)], axis=0)
    mod = _run_mod(cc, w_ada, b_ada.reshape(DEPTH, 1, N_MOD * D_MODEL))
    mod = mod.reshape(DEPTH, MOD_ROWS, 1, N_MOD * D_MODEL)

    ws = gmlp_ws.astype(BF16)
    ws01 = jnp.concatenate([ws[:, 0], ws[:, 1]], axis=2)
    ws23 = jnp.concatenate([ws[:, 2], ws[:, 3]], axis=2)
    bs_tile = jnp.repeat(jnp.swapaxes(gmlp_bs, 1, 2), GMLP_DIM // GMLP_HEADS, axis=2)
    decf = jnp.repeat(ret_decay_fwd.astype(F32), RET_HEAD_DIM, axis=1)[:, None, :]
    decb = jnp.repeat(ret_decay_bwd.astype(F32), RET_HEAD_DIM, axis=1)[:, None, :]
    decf2 = jnp.repeat(ret_decay_fwd.astype(F32), CHUNK, axis=1).reshape(DEPTH, N_PAIR, 1, 2 * CHUNK)
    decb2 = jnp.repeat(ret_decay_bwd.astype(F32), CHUNK, axis=1).reshape(DEPTH, N_PAIR, 1, 2 * CHUNK)
    mix_tables = (conv_w, gmlp_ln_g[:, None, :], gmlp_ln_b[:, None, :], ws01, ws23, bs_tile, decf, decb)
    dec_consts = (decf, decb, decf2, decb2)

    zero_state = jnp.zeros((bsz, RET_DIM, PAIR_W), F32)
    xc = ctx
    for l in range(DEPTH):
        last = l == DEPTH - 1
        mix_consts = (mod, win_b, *mix_tables)
        ctx_out = _run_mix_in(xc, l, *mix_consts, zero_state, tile=ctx_len, conv_row=ctx_len,
                              mod_row=CTX_MOD_ROW, emit_states=True, states_only=last)
        sf_ctx, sb_ctx = ctx_out[-2:]
        to_cast = ((w_out, l), (w_ff1, l), (w_ff2, l)) + (() if last else ((w_in, l + 1),))
        ycg, qkv, g, sb, wout_b, wff1_b, wff2_b, *win_next = _run_mix_in(
            x, l, *mix_consts, sb_ctx, tile=MIX_IN_TILE, conv_row=GRID_W, mod_row=None, emit_states=False,
            cast_next=to_cast)
        out_consts = (mod, ln_g, ln_b, wout_b, wff1_b, wff2_b, *dec_consts)
        x = _run_mix_out(x, ycg, qkv, g, sb, sf_ctx, l, *out_consts, tile=MIX_OUT_TILE, mod_row=None)
        if not last:
            ycg_c, qkv_c, g_c, sb_c = ctx_out[:4]
            xc = _run_mix_out(xc, ycg_c, qkv_c, g_c, sb_c, zero_state, l, *out_consts,
                              tile=ctx_len, mod_row=CTX_MOD_ROW)
            (win_b,) = win_next
    return x
```

```python
import functools

import jax
import jax.numpy as jnp
from jax import lax
from jax.experimental import pallas as pl
from jax.experimental.pallas import tpu as pltpu

D_MODEL = 1024
DEPTH = 2
GRID_W = 64
CONV_DIM = 256
GMLP_DIM = 256
GMLP_HEADS = 4
CHUNK = 128
RET_DIM = 512
RET_HEADS = 8
RET_HEAD_DIM = 64
RET_SCALE = RET_HEAD_DIM ** -0.5
D_FF = 4 * D_MODEL
N_MOD = 6
LN_EPS = 1e-5
ALPHA = (2 * DEPTH) ** 0.25
D_IN = 3 * CONV_DIM + 2 * GMLP_DIM + 4 * RET_DIM

C_CONV = 0
C_GMLP = 3 * CONV_DIM
C_Q = C_GMLP + 2 * GMLP_DIM
C_K = C_Q + RET_DIM
C_V = C_K + RET_DIM
C_G = C_V + RET_DIM

N_PAIR = RET_HEADS // 2
PAIR_W = 2 * RET_HEAD_DIM
MOD_ROWS = 16
CTX_MOD_ROW = 8
FF_BLOCK = 1024
SUB_ROWS = 256
MIX_IN_TILE = 1024
MIX_OUT_TILE = 512
BF16_SUBLANES = 16

VMEM_LIMIT_BYTES = 56 * 1024 * 1024

F32 = jnp.float32
BF16 = jnp.bfloat16


def _layer_norm(v, g, b):
    mu = jnp.mean(v, axis=-1, keepdims=True)
    d = v - mu
    var = jnp.mean(d * d, axis=-1, keepdims=True)
    return d * lax.rsqrt(var + LN_EPS) * g + b


def _mm(a, b):
    return jnp.dot(a, b, preferred_element_type=F32)


def _mm_nt(a, b):
    return lax.dot_general(a, b, (((1,), (1,)), ((), ())), preferred_element_type=F32)


def _mm_tn(a, b):
    return lax.dot_general(a, b, (((0,), (0,)), ((), ())), preferred_element_type=F32)


def _pair_masks():
    lane = lax.broadcasted_iota(jnp.int32, (1, PAIR_W), 1)
    row = lax.broadcasted_iota(jnp.int32, (PAIR_W, 1), 0)
    lo_lane = lane < RET_HEAD_DIM
    block_diag = (row < RET_HEAD_DIM) == lo_lane
    return lo_lane, block_diag


def _stack_heads(a, lo_lane):
    zero = jnp.zeros_like(a)
    return jnp.concatenate([jnp.where(lo_lane, a, zero), jnp.where(lo_lane, zero, a)], axis=0)


def _mod_kernel(cc_ref, w_ref, b_ref, o_ref):
    s = jax.nn.silu(cc_ref[...]).astype(BF16)
    o_ref[...] = _mm(s, w_ref[...].astype(BF16)) + b_ref[...]


def _run_mod(cc, w_ada, b_ada):
    blk = 2 * D_MODEL
    nblk = (N_MOD * D_MODEL) // blk
    return pl.pallas_call(
        _mod_kernel,
        grid=(DEPTH, nblk),
        in_specs=[
            pl.BlockSpec((MOD_ROWS, D_MODEL), lambda l, j: (0, 0)),
            pl.BlockSpec((None, D_MODEL, blk), lambda l, j: (l, 0, j)),
            pl.BlockSpec((None, 1, blk), lambda l, j: (l, 0, j)),
        ],
        out_specs=pl.BlockSpec((None, MOD_ROWS, blk), lambda l, j: (l, 0, j)),
        out_shape=jax.ShapeDtypeStruct((DEPTH, MOD_ROWS, N_MOD * D_MODEL), F32),
        compiler_params=pltpu.CompilerParams(
            dimension_semantics=("arbitrary", "arbitrary"), vmem_limit_bytes=VMEM_LIMIT_BYTES),
        name="ada_modulation",
    )(cc, w_ada, b_ada)


def _mix_in_kernel(x_ref, mod_ref, win_ref, convw_ref, glng_ref, glnb_ref, ws01_ref, ws23_ref, bs_ref,
                   decf_ref, decb_ref, sb0_ref, *rest, tile, conv_row, emit_states, states_only, n_cast):
    cast_src = cast_dst = ()
    if states_only:
        sf_out_ref, sb_out_ref, sb_scr = rest
    elif emit_states:
        ycg_ref, qkv_ref, g_ref, sb_ref, sf_out_ref, sb_out_ref, sb_scr = rest
    else:
        cast_src = rest[:n_cast]
        ycg_ref, qkv_ref, g_ref, sb_ref = rest[n_cast:n_cast + 4]
        cast_dst = rest[n_cast + 4:2 * n_cast + 4]
        sb_scr = rest[-1]
    n_chunk = tile // CHUNK
    t = pl.program_id(1)

    @pl.when(t == 0)
    def _():
        sb_scr[...] = sb0_ref[...]

    sh_m = mod_ref[:, 0 * D_MODEL:1 * D_MODEL]
    sc_m = mod_ref[:, 1 * D_MODEL:2 * D_MODEL]
    h = (x_ref[...] * (1.0 + sc_m) + sh_m).astype(BF16)

    if states_only:
        kv = _mm(h, win_ref[:, C_K:C_K + 2 * RET_DIM])
        k = kv[:, 0:RET_DIM]
        v_bf = kv[:, RET_DIM:2 * RET_DIM].astype(BF16)
    else:
        proj = _mm(h, win_ref[...])

        for src_ref, dst_ref in zip(cast_src, cast_dst):
            dst_ref[...] = src_ref[...].astype(BF16)

        xin = proj[:, C_CONV:C_CONV + CONV_DIM]
        gate_b = proj[:, C_CONV + CONV_DIM:C_CONV + 2 * CONV_DIM]
        gate_c = proj[:, C_CONV + 2 * CONV_DIM:C_CONV + 3 * CONV_DIM]
        z = gate_c * xin
        pos_in_row = lax.broadcasted_iota(jnp.int32, (tile, 1), 0) & (conv_row - 1)
        z_prev = jnp.where(pos_in_row == 0, 0.0, pltpu.roll(z, 1, 0))
        z_next = jnp.where(pos_in_row == conv_row - 1, 0.0, pltpu.roll(z, tile - 1, 0))
        y_conv = gate_b * (convw_ref[0:1, :] * z_prev + convw_ref[1:2, :] * z + convw_ref[2:3, :] * z_next)
        ycg_ref[:, 0:CONV_DIM] = y_conv.astype(BF16)

        u_act = jax.nn.gelu(proj[:, C_GMLP:C_GMLP + GMLP_DIM])
        v_ln = _layer_norm(jax.nn.gelu(proj[:, C_GMLP + GMLP_DIM:C_GMLP + 2 * GMLP_DIM]),
                           glng_ref[...], glnb_ref[...]).astype(BF16)
        lane = lax.broadcasted_iota(jnp.int32, (1, GMLP_DIM), 1)
        head_w = GMLP_DIM // GMLP_HEADS
        in_head = [(lane >= hh * head_w) & (lane < (hh + 1) * head_w) for hh in range(GMLP_HEADS)]
        zero_bf = jnp.zeros((CHUNK, GMLP_DIM), BF16)
        for c in range(n_chunk):
            rows = slice(c * CHUNK, (c + 1) * CHUNK)
            vc = v_ln[rows, :]
            by_head = [jnp.where(in_head[hh], vc, zero_bf) for hh in range(GMLP_HEADS)]
            mixed = (_mm(ws01_ref[...], jnp.concatenate(by_head[0:2], axis=0))
                     + _mm(ws23_ref[...], jnp.concatenate(by_head[2:4], axis=0)) + bs_ref[...])
            ycg_ref[rows, CONV_DIM:CONV_DIM + GMLP_DIM] = (u_act[rows, :] * mixed).astype(BF16)

        qkv_ref[:, 0:RET_DIM] = (proj[:, C_Q:C_Q + RET_DIM] * RET_SCALE).astype(BF16)
        k = proj[:, C_K:C_K + RET_DIM]
        qkv_ref[:, RET_DIM:2 * RET_DIM] = k.astype(BF16)
        v_bf = proj[:, C_V:C_V + RET_DIM].astype(BF16)
        qkv_ref[:, 2 * RET_DIM:3 * RET_DIM] = v_bf
        g_ref[...] = proj[:, C_G:C_G + RET_DIM]

    _, block_diag = _pair_masks()
    pos = lax.broadcasted_iota(jnp.int32, (CHUNK, RET_DIM), 0).astype(F32)
    lg_b = -jnp.exp(decb_ref[...])
    w_b = jnp.exp(lg_b * pos)
    decay_chunk_b = jnp.exp(lg_b * float(CHUNK))
    if emit_states:
        lg_f = -jnp.exp(decf_ref[...])
        w_f = jnp.exp(lg_f * (float(CHUNK - 1) - pos))
        decay_chunk_f = jnp.exp(lg_f * float(CHUNK))
    for c in reversed(range(n_chunk)):
        rows = slice(c * CHUNK, (c + 1) * CHUNK)
        for p in range(N_PAIR):
            cols = slice(p * PAIR_W, (p + 1) * PAIR_W)
            s_old = sb_scr[cols, :]
            if not states_only:
                sb_ref[c, cols, :] = s_old.astype(BF16)
            kw = (k[rows, cols] * w_b[:, cols]).astype(BF16)
            upd = _mm_tn(kw, v_bf[rows, cols])
            sb_scr[cols, :] = s_old * decay_chunk_b[:, cols] + jnp.where(block_diag, upd, 0.0)

    if emit_states:
        sb_out_ref[...] = sb_scr[...]
        for p in range(N_PAIR):
            cols = slice(p * PAIR_W, (p + 1) * PAIR_W)
            s_f = jnp.zeros((PAIR_W, PAIR_W), F32)
            for c in range(n_chunk):
                rows = slice(c * CHUNK, (c + 1) * CHUNK)
                kw = (k[rows, cols] * w_f[:, cols]).astype(BF16)
                upd = _mm_tn(kw, v_bf[rows, cols])
                s_f = s_f * decay_chunk_f[:, cols] + jnp.where(block_diag, upd, 0.0)
            sf_out_ref[cols, :] = s_f


def _run_mix_in(x, layer, mod, win, convw, glng, glnb, ws01, ws23, bs_tile, decf, decb, sb0, *,
                tile, conv_row, mod_row, emit_states, states_only=False, cast_next=()):
    bsz, n, _ = x.shape
    nt = n // tile
    n_chunk = tile // CHUNK
    if emit_states:
        assert nt == 1
    assert emit_states or not states_only
    assert not (cast_next and emit_states)
    rev = lambda b, t: (b, nt - 1 - t, 0)
    per_layer = lambda b, t: (layer, 0, 0)
    mod_map = (lambda b, t: (layer, b, 0, 0)) if mod_row is None else (lambda b, t: (layer, mod_row, 0, 0))
    state_map = lambda b, t: (b, 0, 0)
    out_shape, out_specs = [], []
    if not states_only:
        out_shape += [
            jax.ShapeDtypeStruct((bsz, n, CONV_DIM + GMLP_DIM), BF16),
            jax.ShapeDtypeStruct((bsz, n, 3 * RET_DIM), BF16),
            jax.ShapeDtypeStruct((bsz, n, RET_DIM), F32),
            jax.ShapeDtypeStruct((bsz, n // CHUNK, RET_DIM, PAIR_W), BF16),
        ]
        out_specs += [
            pl.BlockSpec((None, tile, CONV_DIM + GMLP_DIM), rev),
            pl.BlockSpec((None, tile, 3 * RET_DIM), rev),
            pl.BlockSpec((None, tile, RET_DIM), rev),
            pl.BlockSpec((None, n_chunk, RET_DIM, PAIR_W), lambda b, t: (b, nt - 1 - t, 0, 0)),
        ]
    if emit_states:
        out_shape += [jax.ShapeDtypeStruct((bsz, RET_DIM, PAIR_W), F32)] * 2
        out_specs += [pl.BlockSpec((None, RET_DIM, PAIR_W), state_map)] * 2
    extra_in, extra_specs = [], []
    n_steps = bsz * nt
    for w, w_layer in cast_next:
        slab = w.shape[1] // n_steps
        assert w.shape[1] % n_steps == 0 and slab % BF16_SUBLANES == 0
        extra_in.append(w)
        extra_specs.append(
            pl.BlockSpec((None, slab, w.shape[2]), lambda b, t, w_layer=w_layer: (w_layer, b * nt + t, 0)))
        out_shape.append(jax.ShapeDtypeStruct((1,) + w.shape[1:], BF16))
        out_specs.append(pl.BlockSpec((None, slab, w.shape[2]), lambda b, t: (0, b * nt + t, 0)))
    return pl.pallas_call(
        functools.partial(_mix_in_kernel, tile=tile, conv_row=conv_row, emit_states=emit_states,
                          states_only=states_only, n_cast=len(cast_next)),
        grid=(bsz, nt),
        in_specs=[
            pl.BlockSpec((None, tile, D_MODEL), rev),
            pl.BlockSpec((None, None, 1, N_MOD * D_MODEL), mod_map),
            pl.BlockSpec((None, D_MODEL, D_IN), lambda b, t: (0, 0, 0)),
            pl.BlockSpec((None, 3, CONV_DIM), per_layer),
            pl.BlockSpec((None, 1, GMLP_DIM), per_layer),
            pl.BlockSpec((None, 1, GMLP_DIM), per_layer),
            pl.BlockSpec((None, CHUNK, 2 * CHUNK), per_layer),
            pl.BlockSpec((None, CHUNK, 2 * CHUNK), per_layer),
            pl.BlockSpec((None, CHUNK, GMLP_DIM), per_layer),
            pl.BlockSpec((None, 1, RET_DIM), per_layer),
            pl.BlockSpec((None, 1, RET_DIM), per_layer),
            pl.BlockSpec((None, RET_DIM, PAIR_W), state_map),
        ] + extra_specs,
        out_specs=out_specs,
        out_shape=out_shape,
        scratch_shapes=[pltpu.VMEM((RET_DIM, PAIR_W), F32)],
        compiler_params=pltpu.CompilerParams(
            dimension_semantics=("arbitrary", "arbitrary"), vmem_limit_bytes=VMEM_LIMIT_BYTES),
        name="ctx_states" if states_only else ("mix_in_ctx" if emit_states else "mix_in"),
    )(x, mod, win, convw, glng, glnb, ws01, ws23, bs_tile, decf, decb, sb0, *extra_in)


def _mix_out_kernel(x_ref, ycg_ref, qkv_ref, g_ref, sb_ref, sf0_ref, mod_ref, lng_ref, lnb_ref,
                    wout_ref, wff1_ref, wff2_ref, decf_ref, decb_ref, decf2_ref, decb2_ref,
                    o_ref, sf_scr, d2_scr, wq_scr, y_scr, hid_scr, *, tile, sub, tiles_per_seq, n_tiles):
    n_chunk = tile // CHUNK
    s = pl.program_id(0)
    t_ret = lax.rem(jnp.minimum(s, n_tiles - 1), tiles_per_seq)

    lg_f = -jnp.exp(decf_ref[...])
    lg_b = -jnp.exp(decb_ref[...])

    @pl.when(s == 0)
    def _():
        pos = lax.broadcasted_iota(jnp.int32, (CHUNK, RET_DIM), 0).astype(F32)
        wq_scr[0] = jnp.exp(lg_f * (pos + 1.0))
        wq_scr[1] = jnp.exp(lg_b * (float(CHUNK) - pos))
        wq_scr[2] = jnp.exp(lg_f * (float(CHUNK - 1) - pos))
        i = lax.broadcasted_iota(jnp.int32, (CHUNK, 2 * CHUNK), 0)
        j = lax.broadcasted_iota(jnp.int32, (CHUNK, 2 * CHUNK), 1) & (CHUNK - 1)
        rel = i - j
        causal = rel >= 0
        relf = rel.astype(F32)
        for p in range(N_PAIR):
            lf = -jnp.exp(decf2_ref[p])
            lb = -jnp.exp(decb2_ref[p])
            d2_scr[p] = jnp.where(causal, jnp.exp(lf * jnp.where(causal, relf, 0.0)),
                                  jnp.exp(lb * jnp.where(causal, 0.0, -relf)))

    @pl.when(t_ret == 0)
    def _():
        sf_scr[...] = sf0_ref[...]

    lo_lane, block_diag = _pair_masks()
    decay_chunk_f = jnp.exp(lg_f * float(CHUNK))
    inv_d = 1.0 / RET_HEAD_DIM

    gt_m = mod_ref[:, 2 * D_MODEL:3 * D_MODEL]
    sh_f = mod_ref[:, 3 * D_MODEL:4 * D_MODEL]
    sc_f = mod_ref[:, 4 * D_MODEL:5 * D_MODEL]
    gt_f = mod_ref[:, 5 * D_MODEL:6 * D_MODEL]

    units = [(c, p) for c in range(n_chunk) for p in range(N_PAIR)]
    rows = lambda c: slice(c * CHUNK, (c + 1) * CHUNK)
    cols = lambda p: slice(p * PAIR_W, (p + 1) * PAIR_W)
    q_of = lambda c, p: qkv_ref[rows(c), p * PAIR_W:(p + 1) * PAIR_W]
    k_of = lambda c, p: qkv_ref[rows(c), RET_DIM + p * PAIR_W:RET_DIM + (p + 1) * PAIR_W]
    v_of = lambda c, p: qkv_ref[rows(c), 2 * RET_DIM + p * PAIR_W:2 * RET_DIM + (p + 1) * PAIR_W]
    blocks = [slice(i * sub, (i + 1) * sub) for i in range(tile // sub)]

    def step(do_ret, do_mm):
        if do_ret:
            scores = {u: _mm_nt(q_of(*u), _stack_heads(k_of(*u), lo_lane)) for u in units}
            upd = {}
            for c, p in units:
                kw = (k_of(c, p).astype(F32) * wq_scr[2, :, cols(p)]).astype(BF16)
                upd[c, p] = _mm_tn(kw, v_of(c, p))
            cross_b = {(c, p): _mm(q_of(c, p), sb_ref[c, cols(p), :]) for c, p in units}

        if do_mm:
            z = [_mm(y_scr[r, :], wout_ref[...]) for r in blocks]

        if do_ret:
            s_in = {}
            for p in range(N_PAIR):
                s_f = sf_scr[cols(p), :]
                for c in range(n_chunk):
                    s_in[c, p] = s_f
                    s_f = s_f * decay_chunk_f[:, cols(p)] + jnp.where(block_diag, upd[c, p], 0.0)
                sf_scr[cols(p), :] = s_f
            o_in = {(c, p): _mm((scores[c, p] * d2_scr[p]).astype(BF16), _stack_heads(v_of(c, p), lo_lane))
                    for c, p in units}
            cross_f = {u: _mm(q_of(*u), s_in[u].astype(BF16)) for u in units}

        if do_mm:
            for r, z_r in zip(blocks, z):
                x1 = _layer_norm(ALPHA * x_ref[r, :] + gt_m * z_r, lng_ref[0:1, :], lnb_ref[0:1, :])
                o_ref[r, :] = x1
                hf = (x1 * (1.0 + sc_f) + sh_f).astype(BF16)
                for nb in range(D_FF // FF_BLOCK):
                    ff_cols = slice(nb * FF_BLOCK, (nb + 1) * FF_BLOCK)
                    a = jnp.maximum(_mm(hf, wff1_ref[:, ff_cols]), 0.0)
                    hid_scr[r, ff_cols] = (a * a).astype(BF16)

        if do_ret:
            y_scr[:, 0:CONV_DIM + GMLP_DIM] = ycg_ref[...]
            for c, p in units:
                o = o_in[c, p] + wq_scr[0, :, cols(p)] * cross_f[c, p] + wq_scr[1, :, cols(p)] * cross_b[c, p]
                mu = jnp.where(lo_lane,
                               jnp.sum(jnp.where(lo_lane, o, 0.0), axis=-1, keepdims=True),
                               jnp.sum(jnp.where(lo_lane, 0.0, o), axis=-1, keepdims=True)) * inv_d
                d = o - mu
                dd = d * d
                var = jnp.where(lo_lane,
                                jnp.sum(jnp.where(lo_lane, dd, 0.0), axis=-1, keepdims=True),
                                jnp.sum(jnp.where(lo_lane, 0.0, dd), axis=-1, keepdims=True)) * inv_d
                o_n = d * lax.rsqrt(var + LN_EPS)
                y_scr[rows(c), RET_DIM + p * PAIR_W:RET_DIM + (p + 1) * PAIR_W] = (
                    jax.nn.silu(g_ref[rows(c), cols(p)]) * o_n).astype(BF16)

        if do_mm:
            y_ff = [_mm(hid_scr[r, :], wff2_ref[...]) for r in blocks]
            for r, y_r in zip(blocks, y_ff):
                o_ref[r, :] = _layer_norm(ALPHA * o_ref[r, :] + gt_f * y_r, lng_ref[1:2, :], lnb_ref[1:2, :])

    pl.when(s == 0)(lambda: step(True, False))
    pl.when((s > 0) & (s < n_tiles))(lambda: step(True, True))
    pl.when(s == n_tiles)(lambda: step(False, True))


def _run_mix_out(x, ycg, qkv, g, sb, sf0, layer, mod, lng, lnb, wout, wff1, wff2, decf, decb, decf2, decb2, *,
                 tile, mod_row):
    bsz, n, _ = x.shape
    nt = n // tile
    n_tiles = bsz * nt
    n_chunk = tile // CHUNK

    def ret_tile(s):
        i = jnp.minimum(s, n_tiles - 1)
        return lax.div(i, nt), lax.rem(i, nt)

    def mm_tile(s):
        i = jnp.maximum(s - 1, 0)
        return lax.div(i, nt), lax.rem(i, nt)

    ret3 = lambda s: (*ret_tile(s), 0)
    mm3 = lambda s: (*mm_tile(s), 0)
    per_layer = lambda s: (layer, 0, 0)
    per_layer4 = lambda s: (layer, 0, 0, 0)
    this_layer = lambda s: (0, 0, 0)
    mod_map = ((lambda s: (layer, mm_tile(s)[0], 0, 0)) if mod_row is None
               else (lambda s: (layer, mod_row, 0, 0)))
    resident = dict(pipeline_mode=pl.Buffered(1))
    return pl.pallas_call(
        functools.partial(_mix_out_kernel, tile=tile, sub=min(tile, SUB_ROWS), tiles_per_seq=nt, n_tiles=n_tiles),
        grid=(n_tiles + 1,),
        in_specs=[
            pl.BlockSpec((None, tile, D_MODEL), mm3),
            pl.BlockSpec((None, tile, CONV_DIM + GMLP_DIM), ret3),
            pl.BlockSpec((None, tile, 3 * RET_DIM), ret3),
            pl.BlockSpec((None, tile, RET_DIM), ret3),
            pl.BlockSpec((None, n_chunk, RET_DIM, PAIR_W), lambda s: (*ret_tile(s), 0, 0)),
            pl.BlockSpec((None, RET_DIM, PAIR_W), lambda s: (ret_tile(s)[0], 0, 0)),
            pl.BlockSpec((None, None, 1, N_MOD * D_MODEL), mod_map),
            pl.BlockSpec((None, 2, D_MODEL), per_layer),
            pl.BlockSpec((None, 2, D_MODEL), per_layer),
            pl.BlockSpec((None, D_MODEL, D_MODEL), this_layer, **resident),
            pl.BlockSpec((None, D_MODEL, D_FF), this_layer, **resident),
            pl.BlockSpec((None, D_FF, D_MODEL), this_layer, **resident),
            pl.BlockSpec((None, 1, RET_DIM), per_layer),
            pl.BlockSpec((None, 1, RET_DIM), per_layer),
            pl.BlockSpec((None, N_PAIR, 1, 2 * CHUNK), per_layer4),
            pl.BlockSpec((None, N_PAIR, 1, 2 * CHUNK), per_layer4),
        ],
        out_specs=pl.BlockSpec((None, tile, D_MODEL), mm3),
        out_shape=jax.ShapeDtypeStruct((bsz, n, D_MODEL), F32),
        scratch_shapes=[
            pltpu.VMEM((RET_DIM, PAIR_W), F32),
            pltpu.VMEM((N_PAIR, CHUNK, 2 * CHUNK), F32),
            pltpu.VMEM((3, CHUNK, RET_DIM), F32),
            pltpu.VMEM((tile, D_MODEL), BF16),
            pltpu.VMEM((tile, D_FF), BF16),
        ],
        compiler_params=pltpu.CompilerParams(
            dimension_semantics=("arbitrary",), vmem_limit_bytes=VMEM_LIMIT_BYTES),
        name="mix_out",
    )(x, ycg, qkv, g, sb, sf0, mod, lng, lnb, wout, wff1, wff2, decf, decb, decf2, decb2)


def kernel(x, c, ctx, c_ctx, w_ada, b_ada, w_in, conv_w, gmlp_ln_g, gmlp_ln_b, gmlp_ws, gmlp_bs,
           ret_decay_fwd, ret_decay_bwd, w_out, w_ff1, w_ff2, ln_g, ln_b):
    bsz, n_lat, _ = x.shape
    ctx_len = ctx.shape[1]
    assert n_lat % MIX_IN_TILE == 0 and MIX_IN_TILE % CHUNK == 0 and ctx_len % (2 * CHUNK) == 0
    assert n_lat % MIX_OUT_TILE == 0 and MIX_OUT_TILE % (2 * CHUNK) == 0
    assert ctx_len & (ctx_len - 1) == 0 and GRID_W & (GRID_W - 1) == 0

    win_b = w_in[0:1].astype(BF16)

    cc = jnp.concatenate([c, c_ctx[None, :], jnp.zeros((MOD_ROWS - bsz - 1, D_MODEL), F32)], axis=0)
    mod = _run_mod(cc, w_ada, b_ada.reshape(DEPTH, 1, N_MOD * D_MODEL))
    mod = mod.reshape(DEPTH, MOD_ROWS, 1, N_MOD * D_MODEL)

    ws = gmlp_ws.astype(BF16)
    ws01 = jnp.concatenate([ws[:, 0], ws[:, 1]], axis=2)
    ws23 = jnp.concatenate([ws[:, 2], ws[:, 3]], axis=2)
    bs_tile = jnp.repeat(jnp.swapaxes(gmlp_bs, 1, 2), GMLP_DIM // GMLP_HEADS, axis=2)
    decf = jnp.repeat(ret_decay_fwd.astype(F32), RET_HEAD_DIM, axis=1)[:, None, :]
    decb = jnp.repeat(ret_decay_bwd.astype(F32), RET_HEAD_DIM, axis=1)[:, None, :]
    decf2 = jnp.repeat(ret_decay_fwd.astype(F32), CHUNK, axis=1).reshape(DEPTH, N_PAIR, 1, 2 * CHUNK)
    decb2 = jnp.repeat(ret_decay_bwd.astype(F32), CHUNK, axis=1).reshape(DEPTH, N_PAIR, 1, 2 * CHUNK)
    mix_tables = (conv_w, gmlp_ln_g[:, None, :], gmlp_ln_b[:, None, :], ws01, ws23, bs_tile, decf, decb)
    dec_consts = (decf, decb, decf2, decb2)

    zero_state = jnp.zeros((bsz, RET_DIM, PAIR_W), F32)
    xc = ctx
    for l in range(DEPTH):
        last = l == DEPTH - 1
        mix_consts = (mod, win_b, *mix_tables)
        ctx_out = _run_mix_in(xc, l, *mix_consts, zero_state, tile=ctx_len, conv_row=ctx_len,
                              mod_row=CTX_MOD_ROW, emit_states=True, states_only=last)
        sf_ctx, sb_ctx = ctx_out[-2:]
        to_cast = ((w_out, l), (w_ff1, l), (w_ff2, l)) + (() if last else ((w_in, l + 1),))
        ycg, qkv, g, sb, wout_b, wff1_b, wff2_b, *win_next = _run_mix_in(
            x, l, *mix_consts, sb_ctx, tile=MIX_IN_TILE, conv_row=GRID_W, mod_row=None, emit_states=False,
            cast_next=to_cast)
        out_consts = (mod, ln_g, ln_b, wout_b, wff1_b, wff2_b, *dec_consts)
        x = _run_mix_out(x, ycg, qkv, g, sb, sf_ctx, l, *out_consts, tile=MIX_OUT_TILE, mod_row=None)
        if not last:
            ycg_c, qkv_c, g_c, sb_c = ctx_out[:4]
            xc = _run_mix_out(xc, ycg_c, qkv_c, g_c, sb_c, zero_state, l, *out_consts,
                              tile=ctx_len, mod_row=CTX_MOD_ROW)
            (win_b,) = win_next
    return x
```

```python
import functools

import jax
import jax.numpy as jnp
from jax import lax
from jax.experimental import pallas as pl
from jax.experimental.pallas import tpu as pltpu

D_MODEL = 1024
DEPTH = 2
GRID_W = 64
CONV_DIM = 256
GMLP_DIM = 256
GMLP_HEADS = 4
CHUNK = 128
RET_DIM = 512
RET_HEADS = 8
RET_HEAD_DIM = 64
RET_SCALE = RET_HEAD_DIM ** -0.5
D_FF = 4 * D_MODEL
N_MOD = 6
LN_EPS = 1e-5
ALPHA = (2 * DEPTH) ** 0.25
D_IN = 3 * CONV_DIM + 2 * GMLP_DIM + 4 * RET_DIM

C_CONV = 0
C_GMLP = 3 * CONV_DIM
C_Q = C_GMLP + 2 * GMLP_DIM
C_K = C_Q + RET_DIM
C_V = C_K + RET_DIM
C_G = C_V + RET_DIM

N_PAIR = RET_HEADS // 2
PAIR_W = 2 * RET_HEAD_DIM
MOD_ROWS = 16
CTX_MOD_ROW = 8
FF_BLOCK = 1024
SUB_ROWS = 256
MIX_IN_TILE = 1024
MIX_OUT_TILE = 512
BF16_SUBLANES = 16

VMEM_LIMIT_BYTES = 56 * 1024 * 1024

F32 = jnp.float32
BF16 = jnp.bfloat16


def _layer_norm(v, g, b):
    mu = jnp.mean(v, axis=-1, keepdims=True)
    d = v - mu
    var = jnp.mean(d * d, axis=-1, keepdims=True)
    return d * lax.rsqrt(var + LN_EPS) * g + b


def _mm(a, b):
    return jnp.dot(a, b, preferred_element_type=F32)


def _mm_nt(a, b):
    return lax.dot_general(a, b, (((1,), (1,)), ((), ())), preferred_element_type=F32)


def _mm_tn(a, b):
    return lax.dot_general(a, b, (((0,), (0,)), ((), ())), preferred_element_type=F32)


def _pair_masks():
    lane = lax.broadcasted_iota(jnp.int32, (1, PAIR_W), 1)
    row = lax.broadcasted_iota(jnp.int32, (PAIR_W, 1), 0)
    lo_lane = lane < RET_HEAD_DIM
    block_diag = (row < RET_HEAD_DIM) == lo_lane
    return lo_lane, block_diag


def _stack_heads(a, lo_lane):
    zero = jnp.zeros_like(a)
    return jnp.concatenate([jnp.where(lo_lane, a, zero), jnp.where(lo_lane, zero, a)], axis=0)


def _mod_kernel(cc_ref, w_ref, b_ref, o_ref):
    s = jax.nn.silu(cc_ref[...]).astype(BF16)
    o_ref[...] = _mm(s, w_ref[...].astype(BF16)) + b_ref[...]


def _run_mod(cc, w_ada, b_ada):
    blk = 3 * D_MODEL
    nblk = (N_MOD * D_MODEL) // blk
    return pl.pallas_call(
        _mod_kernel,
        grid=(DEPTH, nblk),
        in_specs=[
            pl.BlockSpec((MOD_ROWS, D_MODEL), lambda l, j: (0, 0)),
            pl.BlockSpec((None, D_MODEL, blk), lambda l, j: (l, 0, j)),
            pl.BlockSpec((None, 1, blk), lambda l, j: (l, 0, j)),
        ],
        out_specs=pl.BlockSpec((None, MOD_ROWS, blk), lambda l, j: (l, 0, j)),
        out_shape=jax.ShapeDtypeStruct((DEPTH, MOD_ROWS, N_MOD * D_MODEL), F32),
        compiler_params=pltpu.CompilerParams(
            dimension_semantics=("arbitrary", "arbitrary"), vmem_limit_bytes=VMEM_LIMIT_BYTES),
        name="ada_modulation",
    )(cc, w_ada, b_ada)


def _mix_in_kernel(x_ref, mod_ref, win_ref, convw_ref, glng_ref, glnb_ref, ws01_ref, ws23_ref, bs_ref,
                   decf_ref, decb_ref, sb0_ref, *rest, tile, conv_row, emit_states, states_only, n_cast):
    cast_src = cast_dst = ()
    if states_only:
        sf_out_ref, sb_out_ref, sb_scr = rest
    elif emit_states:
        ycg_ref, qkv_ref, g_ref, sb_ref, sf_out_ref, sb_out_ref, sb_scr = rest
    else:
        cast_src = rest[:n_cast]
        ycg_ref, qkv_ref, g_ref, sb_ref = rest[n_cast:n_cast + 4]
        cast_dst = rest[n_cast + 4:2 * n_cast + 4]
        sb_scr = rest[-1]
    n_chunk = tile // CHUNK
    t = pl.program_id(1)

    @pl.when(t == 0)
    def _():
        sb_scr[...] = sb0_ref[...]

    sh_m = mod_ref[:, 0 * D_MODEL:1 * D_MODEL]
    sc_m = mod_ref[:, 1 * D_MODEL:2 * D_MODEL]
    h = (x_ref[...] * (1.0 + sc_m) + sh_m).astype(BF16)

    if states_only:
        kv = _mm(h, win_ref[:, C_K:C_K + 2 * RET_DIM])
        k = kv[:, 0:RET_DIM]
        v_bf = kv[:, RET_DIM:2 * RET_DIM].astype(BF16)
    else:
        proj = _mm(h, win_ref[...])

        for src_ref, dst_ref in zip(cast_src, cast_dst):
            dst_ref[...] = src_ref[...].astype(BF16)

        xin = proj[:, C_CONV:C_CONV + CONV_DIM]
        gate_b = proj[:, C_CONV + CONV_DIM:C_CONV + 2 * CONV_DIM]
        gate_c = proj[:, C_CONV + 2 * CONV_DIM:C_CONV + 3 * CONV_DIM]
        z = gate_c * xin
        pos_in_row = lax.broadcasted_iota(jnp.int32, (tile, 1), 0) & (conv_row - 1)
        z_prev = jnp.where(pos_in_row == 0, 0.0, pltpu.roll(z, 1, 0))
        z_next = jnp.where(pos_in_row == conv_row - 1, 0.0, pltpu.roll(z, tile - 1, 0))
        y_conv = gate_b * (convw_ref[0:1, :] * z_prev + convw_ref[1:2, :] * z + convw_ref[2:3, :] * z_next)
        ycg_ref[:, 0:CONV_DIM] = y_conv.astype(BF16)

        u_act = jax.nn.gelu(proj[:, C_GMLP:C_GMLP + GMLP_DIM])
        v_ln = _layer_norm(jax.nn.gelu(proj[:, C_GMLP + GMLP_DIM:C_GMLP + 2 * GMLP_DIM]),
                           glng_ref[...], glnb_ref[...]).astype(BF16)
        lane = lax.broadcasted_iota(jnp.int32, (1, GMLP_DIM), 1)
        head_w = GMLP_DIM // GMLP_HEADS
        in_head = [(lane >= hh * head_w) & (lane < (hh + 1) * head_w) for hh in range(GMLP_HEADS)]
        zero_bf = jnp.zeros((CHUNK, GMLP_DIM), BF16)
        for c in range(n_chunk):
            rows = slice(c * CHUNK, (c + 1) * CHUNK)
            vc = v_ln[rows, :]
            by_head = [jnp.where(in_head[hh], vc, zero_bf) for hh in range(GMLP_HEADS)]
            mixed = (_mm(ws01_ref[...], jnp.concatenate(by_head[0:2], axis=0))
                     + _mm(ws23_ref[...], jnp.concatenate(by_head[2:4], axis=0)) + bs_ref[...])
            ycg_ref[rows, CONV_DIM:CONV_DIM + GMLP_DIM] = (u_act[rows, :] * mixed).astype(BF16)

        qkv_ref[:, 0:RET_DIM] = (proj[:, C_Q:C_Q + RET_DIM] * RET_SCALE).astype(BF16)
        k = proj[:, C_K:C_K + RET_DIM]
        qkv_ref[:, RET_DIM:2 * RET_DIM] = k.astype(BF16)
        v_bf = proj[:, C_V:C_V + RET_DIM].astype(BF16)
        qkv_ref[:, 2 * RET_DIM:3 * RET_DIM] = v_bf
        g_ref[...] = proj[:, C_G:C_G + RET_DIM]

    _, block_diag = _pair_masks()
    pos = lax.broadcasted_iota(jnp.int32, (CHUNK, RET_DIM), 0).astype(F32)
    lg_b = -jnp.exp(decb_ref[...])
    w_b = jnp.exp(lg_b * pos)
    decay_chunk_b = jnp.exp(lg_b * float(CHUNK))
    if emit_states:
        lg_f = -jnp.exp(decf_ref[...])
        w_f = jnp.exp(lg_f * (float(CHUNK - 1) - pos))
        decay_chunk_f = jnp.exp(lg_f * float(CHUNK))
    for c in reversed(range(n_chunk)):
        rows = slice(c * CHUNK, (c + 1) * CHUNK)
        for p in range(N_PAIR):
            cols = slice(p * PAIR_W, (p + 1) * PAIR_W)
            s_old = sb_scr[cols, :]
            if not states_only:
                sb_ref[c, cols, :] = s_old.astype(BF16)
            kw = (k[rows, cols] * w_b[:, cols]).astype(BF16)
            upd = _mm_tn(kw, v_bf[rows, cols])
            sb_scr[cols, :] = s_old * decay_chunk_b[:, cols] + jnp.where(block_diag, upd, 0.0)

    if emit_states:
        sb_out_ref[...] = sb_scr[...]
        for p in range(N_PAIR):
            cols = slice(p * PAIR_W, (p + 1) * PAIR_W)
            s_f = jnp.zeros((PAIR_W, PAIR_W), F32)
            for c in range(n_chunk):
                rows = slice(c * CHUNK, (c + 1) * CHUNK)
                kw = (k[rows, cols] * w_f[:, cols]).astype(BF16)
                upd = _mm_tn(kw, v_bf[rows, cols])
                s_f = s_f * decay_chunk_f[:, cols] + jnp.where(block_diag, upd, 0.0)
            sf_out_ref[cols, :] = s_f


def _run_mix_in(x, layer, mod, win, convw, glng, glnb, ws01, ws23, bs_tile, decf, decb, sb0, *,
                tile, conv_row, mod_row, emit_states, states_only=False, cast_next=()):
    bsz, n, _ = x.shape
    nt = n // tile
    n_chunk = tile // CHUNK
    if emit_states:
        assert nt == 1
    assert emit_states or not states_only
    assert not (cast_next and emit_states)
    rev = lambda b, t: (b, nt - 1 - t, 0)
    per_layer = lambda b, t: (layer, 0, 0)
    mod_map = (lambda b, t: (layer, b, 0, 0)) if mod_row is None else (lambda b, t: (layer, mod_row, 0, 0))
    state_map = lambda b, t: (b, 0, 0)
    out_shape, out_specs = [], []
    if not states_only:
        out_shape += [
            jax.ShapeDtypeStruct((bsz, n, CONV_DIM + GMLP_DIM), BF16),
            jax.ShapeDtypeStruct((bsz, n, 3 * RET_DIM), BF16),
            jax.ShapeDtypeStruct((bsz, n, RET_DIM), F32),
            jax.ShapeDtypeStruct((bsz, n // CHUNK, RET_DIM, PAIR_W), BF16),
        ]
        out_specs += [
            pl.BlockSpec((None, tile, CONV_DIM + GMLP_DIM), rev),
            pl.BlockSpec((None, tile, 3 * RET_DIM), rev),
            pl.BlockSpec((None, tile, RET_DIM), rev),
            pl.BlockSpec((None, n_chunk, RET_DIM, PAIR_W), lambda b, t: (b, nt - 1 - t, 0, 0)),
        ]
    if emit_states:
        out_shape += [jax.ShapeDtypeStruct((bsz, RET_DIM, PAIR_W), F32)] * 2
        out_specs += [pl.BlockSpec((None, RET_DIM, PAIR_W), state_map)] * 2
    extra_in, extra_specs = [], []
    n_steps = bsz * nt
    for w, w_layer in cast_next:
        slab = w.shape[1] // n_steps
        assert w.shape[1] % n_steps == 0 and slab % BF16_SUBLANES == 0
        extra_in.append(w)
        extra_specs.append(
            pl.BlockSpec((None, slab, w.shape[2]), lambda b, t, w_layer=w_layer: (w_layer, b * nt + t, 0)))
        out_shape.append(jax.ShapeDtypeStruct((1,) + w.shape[1:], BF16))
        out_specs.append(pl.BlockSpec((None, slab, w.shape[2]), lambda b, t: (0, b * nt + t, 0)))
    return pl.pallas_call(
        functools.partial(_mix_in_kernel, tile=tile, conv_row=conv_row, emit_states=emit_states,
                          states_only=states_only, n_cast=len(cast_next)),
        grid=(bsz, nt),
        in_specs=[
            pl.BlockSpec((None, tile, D_MODEL), rev),
            pl.BlockSpec((None, None, 1, N_MOD * D_MODEL), mod_map),
            pl.BlockSpec((None, D_MODEL, D_IN), lambda b, t: (0, 0, 0)),
            pl.BlockSpec((None, 3, CONV_DIM), per_layer),
            pl.BlockSpec((None, 1, GMLP_DIM), per_layer),
            pl.BlockSpec((None, 1, GMLP_DIM), per_layer),
            pl.BlockSpec((None, CHUNK, 2 * CHUNK), per_layer),
            pl.BlockSpec((None, CHUNK, 2 * CHUNK), per_layer),
            pl.BlockSpec((None, CHUNK, GMLP_DIM), per_layer),
            pl.BlockSpec((None, 1, RET_DIM), per_layer),
            pl.BlockSpec((None, 1, RET_DIM), per_layer),
            pl.BlockSpec((None, RET_DIM, PAIR_W), state_map),
        ] + extra_specs,
        out_specs=out_specs,
        out_shape=out_shape,
        scratch_shapes=[pltpu.VMEM((RET_DIM, PAIR_W), F32)],
        compiler_params=pltpu.CompilerParams(
            dimension_semantics=("arbitrary", "arbitrary"), vmem_limit_bytes=VMEM_LIMIT_BYTES),
        name="ctx_states" if states_only else ("mix_in_ctx" if emit_states else "mix_in"),
    )(x, mod, win, convw, glng, glnb, ws01, ws23, bs_tile, decf, decb, sb0, *extra_in)


def _mix_out_kernel(x_ref, ycg_ref, qkv_ref, g_ref, sb_ref, sf0_ref, mod_ref, lng_ref, lnb_ref,
                    wout_ref, wff1_ref, wff2_ref, decf_ref, decb_ref, decf2_ref, decb2_ref,
                    o_ref, sf_scr, d2_scr, wq_scr, y_scr, hid_scr, *, tile, sub, tiles_per_seq, n_tiles):
    n_chunk = tile // CHUNK
    s = pl.program_id(0)
    t_ret = lax.rem(jnp.minimum(s, n_tiles - 1), tiles_per_seq)

    lg_f = -jnp.exp(decf_ref[...])
    lg_b = -jnp.exp(decb_ref[...])

    @pl.when(s == 0)
    def _():
        pos = lax.broadcasted_iota(jnp.int32, (CHUNK, RET_DIM), 0).astype(F32)
        wq_scr[0] = jnp.exp(lg_f * (pos + 1.0))
        wq_scr[1] = jnp.exp(lg_b * (float(CHUNK) - pos))
        wq_scr[2] = jnp.exp(lg_f * (float(CHUNK - 1) - pos))
        i = lax.broadcasted_iota(jnp.int32, (CHUNK, 2 * CHUNK), 0)
        j = lax.broadcasted_iota(jnp.int32, (CHUNK, 2 * CHUNK), 1) & (CHUNK - 1)
        rel = i - j
        causal = rel >= 0
        relf = rel.astype(F32)
        for p in range(N_PAIR):
            lf = -jnp.exp(decf2_ref[p])
            lb = -jnp.exp(decb2_ref[p])
            d2_scr[p] = jnp.where(causal, jnp.exp(lf * jnp.where(causal, relf, 0.0)),
                                  jnp.exp(lb * jnp.where(causal, 0.0, -relf)))

    @pl.when(t_ret == 0)
    def _():
        sf_scr[...] = sf0_ref[...]

    lo_lane, block_diag = _pair_masks()
    decay_chunk_f = jnp.exp(lg_f * float(CHUNK))
    inv_d = 1.0 / RET_HEAD_DIM

    gt_m = mod_ref[:, 2 * D_MODEL:3 * D_MODEL]
    sh_f = mod_ref[:, 3 * D_MODEL:4 * D_MODEL]
    sc_f = mod_ref[:, 4 * D_MODEL:5 * D_MODEL]
    gt_f = mod_ref[:, 5 * D_MODEL:6 * D_MODEL]

    units = [(c, p) for c in range(n_chunk) for p in range(N_PAIR)]
    rows = lambda c: slice(c * CHUNK, (c + 1) * CHUNK)
    cols = lambda p: slice(p * PAIR_W, (p + 1) * PAIR_W)
    q_of = lambda c, p: qkv_ref[rows(c), p * PAIR_W:(p + 1) * PAIR_W]
    k_of = lambda c, p: qkv_ref[rows(c), RET_DIM + p * PAIR_W:RET_DIM + (p + 1) * PAIR_W]
    v_of = lambda c, p: qkv_ref[rows(c), 2 * RET_DIM + p * PAIR_W:2 * RET_DIM + (p + 1) * PAIR_W]
    blocks = [slice(i * sub, (i + 1) * sub) for i in range(tile // sub)]

    def step(do_ret, do_mm):
        if do_ret:
            scores = {u: _mm_nt(q_of(*u), _stack_heads(k_of(*u), lo_lane)) for u in units}
            upd = {}
            for c, p in units:
                kw = (k_of(c, p).astype(F32) * wq_scr[2, :, cols(p)]).astype(BF16)
                upd[c, p] = _mm_tn(kw, v_of(c, p))
            cross_b = {(c, p): _mm(q_of(c, p), sb_ref[c, cols(p), :]) for c, p in units}

        if do_mm:
            z = [_mm(y_scr[r, :], wout_ref[...]) for r in blocks]

        if do_ret:
            s_in = {}
            for p in range(N_PAIR):
                s_f = sf_scr[cols(p), :]
                for c in range(n_chunk):
                    s_in[c, p] = s_f
                    s_f = s_f * decay_chunk_f[:, cols(p)] + jnp.where(block_diag, upd[c, p], 0.0)
                sf_scr[cols(p), :] = s_f
            o_in = {(c, p): _mm((scores[c, p] * d2_scr[p]).astype(BF16), _stack_heads(v_of(c, p), lo_lane))
                    for c, p in units}
            cross_f = {u: _mm(q_of(*u), s_in[u].astype(BF16)) for u in units}

        if do_mm:
            for r, z_r in zip(blocks, z):
                x1 = _layer_norm(ALPHA * x_ref[r, :] + gt_m * z_r, lng_ref[0:1, :], lnb_ref[0:1, :])
                o_ref[r, :] = x1
                hf = (x1 * (1.0 + sc_f) + sh_f).astype(BF16)
                for nb in range(D_FF // FF_BLOCK):
                    ff_cols = slice(nb * FF_BLOCK, (nb + 1) * FF_BLOCK)
                    a = jnp.maximum(_mm(hf, wff1_ref[:, ff_cols]), 0.0)
                    hid_scr[r, ff_cols] = (a * a).astype(BF16)

        if do_ret:
            y_scr[:, 0:CONV_DIM + GMLP_DIM] = ycg_ref[...]
            for c, p in units:
                o = o_in[c, p] + wq_scr[0, :, cols(p)] * cross_f[c, p] + wq_scr[1, :, cols(p)] * cross_b[c, p]
                mu = jnp.where(lo_lane,
                               jnp.sum(jnp.where(lo_lane, o, 0.0), axis=-1, keepdims=True),
                               jnp.sum(jnp.where(lo_lane, 0.0, o), axis=-1, keepdims=True)) * inv_d
                d = o - mu
                dd = d * d
                var = jnp.where(lo_lane,
                                jnp.sum(jnp.where(lo_lane, dd, 0.0), axis=-1, keepdims=True),
                                jnp.sum(jnp.where(lo_lane, 0.0, dd), axis=-1, keepdims=True)) * inv_d
                o_n = d * lax.rsqrt(var + LN_EPS)
                y_scr[rows(c), RET_DIM + p * PAIR_W:RET_DIM + (p + 1) * PAIR_W] = (
                    jax.nn.silu(g_ref[rows(c), cols(p)]) * o_n).astype(BF16)

        if do_mm:
            y_ff = [_mm(hid_scr[r, :], wff2_ref[...]) for r in blocks]
            for r, y_r in zip(blocks, y_ff):
                o_ref[r, :] = _layer_norm(ALPHA * o_ref[r, :] + gt_f * y_r, lng_ref[1:2, :], lnb_ref[1:2, :])

    pl.when(s == 0)(lambda: step(True, False))
    pl.when((s > 0) & (s < n_tiles))(lambda: step(True, True))
    pl.when(s == n_tiles)(lambda: step(False, True))


def _run_mix_out(x, ycg, qkv, g, sb, sf0, layer, mod, lng, lnb, wout, wff1, wff2, decf, decb, decf2, decb2, *,
                 tile, mod_row):
    bsz, n, _ = x.shape
    nt = n // tile
    n_tiles = bsz * nt
    n_chunk = tile // CHUNK

    def ret_tile(s):
        i = jnp.minimum(s, n_tiles - 1)
        return lax.div(i, nt), lax.rem(i, nt)

    def mm_tile(s):
        i = jnp.maximum(s - 1, 0)
        return lax.div(i, nt), lax.rem(i, nt)

    ret3 = lambda s: (*ret_tile(s), 0)
    mm3 = lambda s: (*mm_tile(s), 0)
    per_layer = lambda s: (layer, 0, 0)
    per_layer4 = lambda s: (layer, 0, 0, 0)
    this_layer = lambda s: (0, 0, 0)
    mod_map = ((lambda s: (layer, mm_tile(s)[0], 0, 0)) if mod_row is None
               else (lambda s: (layer, mod_row, 0, 0)))
    resident = dict(pipeline_mode=pl.Buffered(1))
    return pl.pallas_call(
        functools.partial(_mix_out_kernel, tile=tile, sub=min(tile, SUB_ROWS), tiles_per_seq=nt, n_tiles=n_tiles),
        grid=(n_tiles + 1,),
        in_specs=[
            pl.BlockSpec((None, tile, D_MODEL), mm3),
            pl.BlockSpec((None, tile, CONV_DIM + GMLP_DIM), ret3),
            pl.BlockSpec((None, tile, 3 * RET_DIM), ret3),
            pl.BlockSpec((None, tile, RET_DIM), ret3),
            pl.BlockSpec((None, n_chunk, RET_DIM, PAIR_W), lambda s: (*ret_tile(s), 0, 0)),
            pl.BlockSpec((None, RET_DIM, PAIR_W), lambda s: (ret_tile(s)[0], 0, 0)),
            pl.BlockSpec((None, None, 1, N_MOD * D_MODEL), mod_map),
            pl.BlockSpec((None, 2, D_MODEL), per_layer),
            pl.BlockSpec((None, 2, D_MODEL), per_layer),
            pl.BlockSpec((None, D_MODEL, D_MODEL), this_layer, **resident),
            pl.BlockSpec((None, D_MODEL, D_FF), this_layer, **resident),
            pl.BlockSpec((None, D_FF, D_MODEL), this_layer, **resident),
            pl.BlockSpec((None, 1, RET_DIM), per_layer),
            pl.BlockSpec((None, 1, RET_DIM), per_layer),
            pl.BlockSpec((None, N_PAIR, 1, 2 * CHUNK), per_layer4),
            pl.BlockSpec((None, N_PAIR, 1, 2 * CHUNK), per_layer4),
        ],
        out_specs=pl.BlockSpec((None, tile, D_MODEL), mm3),
        out_shape=jax.ShapeDtypeStruct((bsz, n, D_MODEL), F32),
        scratch_shapes=[
            pltpu.VMEM((RET_DIM, PAIR_W), F32),
            pltpu.VMEM((N_PAIR, CHUNK, 2 * CHUNK), F32),
            pltpu.VMEM((3, CHUNK, RET_DIM), F32),
            pltpu.VMEM((tile, D_MODEL), BF16),
            pltpu.VMEM((tile, D_FF), BF16),
        ],
        compiler_params=pltpu.CompilerParams(
            dimension_semantics=("arbitrary",), vmem_limit_bytes=VMEM_LIMIT_BYTES),
        name="mix_out",
    )(x, ycg, qkv, g, sb, sf0, mod, lng, lnb, wout, wff1, wff2, decf, decb, decf2, decb2)


def kernel(x, c, ctx, c_ctx, w_ada, b_ada, w_in, conv_w, gmlp_ln_g, gmlp_ln_b, gmlp_ws, gmlp_bs,
           ret_decay_fwd, ret_decay_bwd, w_out, w_ff1, w_ff2, ln_g, ln_b):
    bsz, n_lat, _ = x.shape
    ctx_len = ctx.shape[1]
    assert n_lat % MIX_IN_TILE == 0 and MIX_IN_TILE % CHUNK == 0 and ctx_len % (2 * CHUNK) == 0
    assert n_lat % MIX_OUT_TILE == 0 and MIX_OUT_TILE % (2 * CHUNK) == 0
    assert ctx_len & (ctx_len - 1) == 0 and GRID_W & (GRID_W - 1) == 0

    win_b = w_in[0:1].astype(BF16)

    cc = jnp.concatenate([c, c_ctx[None, :], jnp.zeros((MOD_ROWS - bsz - 1, D_MODEL), F32)], axis=0)
    mod = _run_mod(cc, w_ada, b_ada.reshape(DEPTH, 1, N_MOD * D_MODEL))
    mod = mod.reshape(DEPTH, MOD_ROWS, 1, N_MOD * D_MODEL)

    ws = gmlp_ws.astype(BF16)
    ws01 = jnp.concatenate([ws[:, 0], ws[:, 1]], axis=2)
    ws23 = jnp.concatenate([ws[:, 2], ws[:, 3]], axis=2)
    bs_tile = jnp.repeat(jnp.swapaxes(gmlp_bs, 1, 2), GMLP_DIM // GMLP_HEADS, axis=2)
    decf = jnp.repeat(ret_decay_fwd.astype(F32), RET_HEAD_DIM, axis=1)[:, None, :]
    decb = jnp.repeat(ret_decay_bwd.astype(F32), RET_HEAD_DIM, axis=1)[:, None, :]
    decf2 = jnp.repeat(ret_decay_fwd.astype(F32), CHUNK, axis=1).reshape(DEPTH, N_PAIR, 1, 2 * CHUNK)
    decb2 = jnp.repeat(ret_decay_bwd.astype(F32), CHUNK, axis=1).reshape(DEPTH, N_PAIR, 1, 2 * CHUNK)
    mix_tables = (conv_w, gmlp_ln_g[:, None, :], gmlp_ln_b[:, None, :], ws01, ws23, bs_tile, decf, decb)
    dec_consts = (decf, decb, decf2, decb2)

    zero_state = jnp.zeros((bsz, RET_DIM, PAIR_W), F32)
    xc = ctx
    for l in range(DEPTH):
        last = l == DEPTH - 1
        mix_consts = (mod, win_b, *mix_tables)
        ctx_out = _run_mix_in(xc, l, *mix_consts, zero_state, tile=ctx_len, conv_row=ctx_len,
                              mod_row=CTX_MOD_ROW, emit_states=True, states_only=last)
        sf_ctx, sb_ctx = ctx_out[-2:]
        to_cast = ((w_out, l), (w_ff1, l), (w_ff2, l)) + (() if last else ((w_in, l + 1),))
        ycg, qkv, g, sb, wout_b, wff1_b, wff2_b, *win_next = _run_mix_in(
            x, l, *mix_consts, sb_ctx, tile=MIX_IN_TILE, conv_row=GRID_W, mod_row=None, emit_states=False,
            cast_next=to_cast)
        out_consts = (mod, ln_g, ln_b, wout_b, wff1_b, wff2_b, *dec_consts)
        if not last:
            ycg_c, qkv_c, g_c, sb_c = ctx_out[:4]
            xc = _run_mix_out(xc, ycg_c, qkv_c, g_c, sb_c, zero_state, l, *out_consts,
                              tile=ctx_len, mod_row=CTX_MOD_ROW)
            (win_b,) = win_next
        x = _run_mix_out(x, ycg, qkv, g, sb, sf_ctx, l, *out_consts, tile=MIX_OUT_TILE, mod_row=None)
    return x
```
